```python
import math
import jax, jax.numpy as jnp
from jax import lax
import numpy as np

D_MODEL = 1024
BATCH = 2
SEQ = 8192
DEPTH = 2

CTX_LEN = 256
GRID_W = 64

MLA_HEADS = 8
QK_NOPE_DIM = 64
QK_ROPE_DIM = 32
QK_HEAD_DIM = QK_NOPE_DIM + QK_ROPE_DIM
V_HEAD_DIM = 64
Q_LORA_RANK = 256
KV_LORA_RANK = 128
MLA_OUT = MLA_HEADS * V_HEAD_DIM

CONV_DIM = 512
CONV_K = 3

IN_SPLITS = [Q_LORA_RANK,
             Q_LORA_RANK + KV_LORA_RANK,
             Q_LORA_RANK + KV_LORA_RANK + QK_ROPE_DIM,
             Q_LORA_RANK + KV_LORA_RANK + QK_ROPE_DIM + CONV_DIM,
             Q_LORA_RANK + KV_LORA_RANK + QK_ROPE_DIM + 2 * CONV_DIM]
D_IN = Q_LORA_RANK + KV_LORA_RANK + QK_ROPE_DIM + 3 * CONV_DIM
KV_COL_START = Q_LORA_RANK
KV_COL_END = Q_LORA_RANK + KV_LORA_RANK + QK_ROPE_DIM
MIX_WIDTH = MLA_OUT + CONV_DIM

D_FF = 2816
N_MOD = 9
ROPE_BASE = 10000.0
EPS = 1e-6
Q_BLOCK = 128
ATTN_SCALE = 1.0 / math.sqrt(QK_HEAD_DIM)

kernel_name = "hybrid_mla_shortconv_macaron_dit"


def rmsnorm(x, w):
    x32 = x.astype(jnp.float32)
    r = lax.rsqrt(jnp.mean(x32 * x32, axis=-1, keepdims=True) + EPS)
    return (x32 * r).astype(x.dtype) * w


def modulate(h, shift, scale):
    return h * (1 + scale) + shift


def swiglu(h, w_i, w_o):
    g, u = jnp.split(h @ w_i, 2, axis=-1)
    return (jax.nn.silu(g) * u) @ w_o


def axial_rope_tables(rows, dtype):
    t = jnp.arange(rows * GRID_W)
    row = (t // GRID_W).astype(jnp.float32)
    col = (t % GRID_W).astype(jnp.float32)
    d_axis = QK_ROPE_DIM // 2
    inv = ROPE_BASE ** (-jnp.arange(0, d_axis, 2, dtype=jnp.float32) / d_axis)
    ar = row[:, None] * inv
    ac = col[:, None] * inv
    return tuple(a.astype(dtype) for a in (jnp.cos(ar), jnp.sin(ar), jnp.cos(ac), jnp.sin(ac)))


def rotate(x, cos, sin):
    x1, x2 = jnp.split(x, 2, axis=-1)
    return jnp.concatenate([x1 * cos - x2 * sin, x2 * cos + x1 * sin], axis=-1)


def axial_rope(x, tabs):
    cr, sr, cc, sc = tabs
    xr, xc = jnp.split(x, 2, axis=-1)
    return jnp.concatenate([rotate(xr, cr, sr), rotate(xc, cc, sc)], axis=-1)


def attend(q, k, v):
    s = jnp.einsum('bqhd,bkhd->bhqk', q, k, preferred_element_type=jnp.float32) * ATTN_SCALE
    p = jax.nn.softmax(s, axis=-1).astype(v.dtype)
    return jnp.einsum('bhqk,bkhd->bqhd', p, v)


def blocked_attend(q, k, v):
    b, l, h, d = q.shape
    nb = l // Q_BLOCK
    qb = q.reshape(b, nb, Q_BLOCK, h, d).transpose(1, 0, 2, 3, 4)
    ob = lax.map(lambda qi: attend(qi, k, v), qb)
    return ob.transpose(1, 0, 2, 3, 4).reshape(b, l, h, v.shape[-1])


def mla_queries(cq, q_norm_w, w_uq, tabs):
    b, l, _ = cq.shape
    q = (rmsnorm(cq, q_norm_w) @ w_uq).reshape(b, l, MLA_HEADS, QK_HEAD_DIM)
    q_nope, q_pe = q[..., :QK_NOPE_DIM], q[..., QK_NOPE_DIM:]
    if tabs is not None:
        q_pe = axial_rope(q_pe, tuple(t[:, None, :] for t in tabs))
    return jnp.concatenate([q_nope, q_pe], axis=-1)


def mla_keys_values(ckv, k_pe, kv_norm_w, w_ukv, tabs):
    b, l, _ = ckv.shape
    kv = (rmsnorm(ckv, kv_norm_w) @ w_ukv).reshape(b, l, MLA_HEADS, QK_NOPE_DIM + V_HEAD_DIM)
    k_nope, v = kv[..., :QK_NOPE_DIM], kv[..., QK_NOPE_DIM:]
    if tabs is not None:
        k_pe = axial_rope(k_pe, tabs)
    k_pe = jnp.broadcast_to(k_pe[:, :, None, :], (b, l, MLA_HEADS, QK_ROPE_DIM))
    return jnp.concatenate([k_nope, k_pe], axis=-1), v


def short_conv(u, w):
    up = jnp.pad(u, ((0, 0), (1, 1), (0, 0)))
    return up[:, :-2] * w[0] + up[:, 1:-1] * w[1] + up[:, 2:] * w[2]


def token_mixer(hl, hc, w_in, q_norm_w, kv_norm_w, w_uq, w_ukv, conv_w, w_out, tabs, ctx_out):
    b, l, _ = hl.shape
    cq_l, ckv_l, kpe_l, gb_l, gc_l, xv_l = jnp.split(hl @ w_in, IN_SPLITS, axis=-1)
    if ctx_out:
        cq_c, ckv_c, kpe_c, gb_c, gc_c, xv_c = jnp.split(hc @ w_in, IN_SPLITS, axis=-1)
    else:
        ckv_c, kpe_c = jnp.split(hc @ w_in[:, KV_COL_START:KV_COL_END], [KV_LORA_RANK], axis=-1)
    k_c, v_c = mla_keys_values(ckv_c, kpe_c, kv_norm_w, w_ukv, None)
    k_l, v_l = mla_keys_values(ckv_l, kpe_l, kv_norm_w, w_ukv, tabs)
    q_l = mla_queries(cq_l, q_norm_w, w_uq, tabs)
    k_all = jnp.concatenate([k_c, k_l], axis=1)
    v_all = jnp.concatenate([v_c, v_l], axis=1)
    att_l = blocked_attend(q_l, k_all, v_all).reshape(b, l, MLA_OUT)
    conv_l = gb_l * short_conv(gc_l * xv_l, conv_w)
    out_l = jnp.concatenate([att_l, conv_l], axis=-1) @ w_out
    if not ctx_out:
        return out_l, None
    q_c = mla_queries(cq_c, q_norm_w, w_uq, None)
    att_c = attend(q_c, k_c, v_c).reshape(hc.shape[0], hc.shape[1], MLA_OUT)
    conv_c = gb_c * short_conv(gc_c * xv_c, conv_w)
    out_c = jnp.concatenate([att_c, conv_c], axis=-1) @ w_out
    return out_l, out_c


def trunk_layer(xl, xc, ml, mc, norm_w, w_ffn1_in, w_ffn1_out, w_ffn2_in, w_ffn2_out,
                w_in, q_norm_w, kv_norm_w, w_uq, w_ukv, conv_w, w_out, tabs, last):
    sh1l, sc1l, g1l, sh2l, sc2l, g2l, sh3l, sc3l, g3l = ml
    sh1c, sc1c, g1c, sh2c, sc2c, g2c, sh3c, sc3c, g3c = mc
    xl = xl + 0.5 * g1l * swiglu(modulate(rmsnorm(xl, norm_w[0]), sh1l, sc1l), w_ffn1_in, w_ffn1_out)
    xc = xc + 0.5 * g1c * swiglu(modulate(rmsnorm(xc, norm_w[0]), sh1c, sc1c), w_ffn1_in, w_ffn1_out)
    hl = modulate(rmsnorm(xl, norm_w[1]), sh2l, sc2l)
    hc = modulate(rmsnorm(xc, norm_w[1]), sh2c, sc2c)
    mix_l, mix_c = token_mixer(hl, hc, w_in, q_norm_w, kv_norm_w, w_uq, w_ukv, conv_w, w_out,
                               tabs, not last)
    xl = xl + g2l * mix_l
    xl = xl + 0.5 * g3l * swiglu(modulate(rmsnorm(xl, norm_w[2]), sh3l, sc3l), w_ffn2_in, w_ffn2_out)
    if not last:
        xc = xc + g2c * mix_c
        xc = xc + 0.5 * g3c * swiglu(modulate(rmsnorm(xc, norm_w[2]), sh3c, sc3c), w_ffn2_in, w_ffn2_out)
    return xl, xc


def setup_inputs(seed: int = 0) -> dict:
    key = jax.random.key(seed)
    ks = jax.random.split(key, 20)
    f32 = jnp.float32

    def dense(k, shape, fan_in, gain=1.0):
        return jax.random.normal(k, shape, f32) * (gain * fan_in ** -0.5)

    def gains(k, shape):
        return 1.0 + 0.05 * jax.random.normal(k, shape, f32)

    return {
        "x": jax.random.normal(ks[0], (BATCH, SEQ, D_MODEL), f32),
        "c": jax.random.normal(ks[1], (BATCH, D_MODEL), f32),
        "ctx": jax.random.normal(ks[2], (BATCH, CTX_LEN, D_MODEL), f32),
        "c_ctx": jax.random.normal(ks[3], (D_MODEL,), f32),
        "w_ada": dense(ks[4], (DEPTH, D_MODEL, N_MOD * D_MODEL), D_MODEL, 0.5),
        "b_ada": 0.01 * jax.random.normal(ks[5], (DEPTH, N_MOD * D_MODEL), f32),
        "norm_w": gains(ks[6], (DEPTH, 3, D_MODEL)),
        "w_ffn1_in": dense(ks[7], (DEPTH, D_MODEL, 2 * D_FF), D_MODEL),
        "w_ffn1_out": dense(ks[8], (DEPTH, D_FF, D_MODEL), D_FF),
        "w_ffn2_in": dense(ks[9], (DEPTH, D_MODEL, 2 * D_FF), D_MODEL),
        "w_ffn2_out": dense(ks[10], (DEPTH, D_FF, D_MODEL), D_FF),
        "w_in": dense(ks[11], (DEPTH, D_MODEL, D_IN), D_MODEL),
        "q_norm_w": gains(ks[12], (DEPTH, Q_LORA_RANK)),
        "kv_norm_w": gains(ks[13], (DEPTH, KV_LORA_RANK)),
        "w_uq": dense(ks[14], (DEPTH, Q_LORA_RANK, MLA_HEADS * QK_HEAD_DIM), Q_LORA_RANK),
        "w_ukv": dense(ks[15], (DEPTH, KV_LORA_RANK, MLA_HEADS * (QK_NOPE_DIM + V_HEAD_DIM)), KV_LORA_RANK),
        "conv_w": dense(ks[16], (DEPTH, CONV_K, CONV_DIM), CONV_K),
        "w_out": dense(ks[17], (DEPTH, MIX_WIDTH, D_MODEL), MIX_WIDTH),
        "final_norm_w": gains(ks[18], (D_MODEL,)),
    }


def reference(x, c, ctx, c_ctx, w_ada, b_ada, norm_w, w_ffn1_in, w_ffn1_out, w_ffn2_in, w_ffn2_out,
              w_in, q_norm_w, kv_norm_w, w_uq, w_ukv, conv_w, w_out, final_norm_w):
    n_tokens = x.shape[1]
    rows = n_tokens // GRID_W
    tabs = axial_rope_tables(rows, x.dtype)
    xl, xc = x, ctx
    for i in range(DEPTH):
        last = i == DEPTH - 1
        ml = jnp.split((jax.nn.silu(c) @ w_ada[i] + b_ada[i])[:, None, :], N_MOD, axis=-1)
        mc = jnp.split(jax.nn.silu(c_ctx) @ w_ada[i] + b_ada[i], N_MOD, axis=-1)
        xl, xc = trunk_layer(xl, xc, ml, mc, norm_w[i], w_ffn1_in[i], w_ffn1_out[i],
                             w_ffn2_in[i], w_ffn2_out[i], w_in[i], q_norm_w[i], kv_norm_w[i],
                             w_uq[i], w_ukv[i], conv_w[i], w_out[i], tabs, last)
    return rmsnorm(xl, final_norm_w)
```

```python
import functools
import math

import jax
import jax.numpy as jnp
from jax import lax
from jax.experimental import pallas as pl
from jax.experimental.pallas import tpu as pltpu

F32 = jnp.float32
BF16 = jnp.bfloat16

D_MODEL = 1024
HEADS = 8
NOPE = 64
ROPE = 32
QK_DIM = NOPE + ROPE
V_DIM = 64
Q_RANK = 256
KV_RANK = 128
CONV_DIM = 512
D_FF = 2816
N_MOD = 9
GRID_W = 64
ROPE_BASE = 10000.0
EPS = 1e-6

HEAD_PAD = 128
V_ROWS = 80
TOKEN_BLOCK = 256
KEY_CHUNK = 768
HALO = 8
ATTN_HEADS_PER_STEP = 2
VMEM_LIMIT_BYTES = 56 * 1024 * 1024

Q_SCALE = math.log2(math.e) / math.sqrt(QK_DIM)

_ROPE_PARTNER = tuple(range(8, 16)) + tuple(range(0, 8)) + tuple(range(24, 32)) + tuple(range(16, 24))


def _dot(a, b):
    return jnp.dot(a, b, preferred_element_type=F32)


def _dot_nt(a, b):
    return lax.dot_general(a, b, (((1,), (1,)), ((), ())), preferred_element_type=F32)


def _rmsnorm(x, w):
    r = lax.rsqrt(jnp.mean(x * x, axis=-1, keepdims=True) + EPS)
    return (x * r) * w


def _const_spec(shape):
    return pl.BlockSpec(shape, lambda *_: (0,) * len(shape), pipeline_mode=pl.Buffered(1))


def _params():
    return pltpu.CompilerParams(
        dimension_semantics=("arbitrary", "arbitrary"), vmem_limit_bytes=VMEM_LIMIT_BYTES)


def _ada_kernel(c_ref, w_ref, b_ref, o_ref):
    c = c_ref[...]
    act = (c * (1.0 / (1.0 + jnp.exp(-c)))).astype(BF16)
    o_ref[0] = _dot(act, w_ref[0].astype(BF16)) + b_ref[0]


def _ada(cond, w_ada, b_ada):
    depth = w_ada.shape[0]
    n_out = w_ada.shape[2]
    blk = 1024
    return pl.pallas_call(
        _ada_kernel,
        grid=(depth, n_out // blk),
        in_specs=[
            pl.BlockSpec((8, D_MODEL), lambda l, j: (0, 0)),
            pl.BlockSpec((1, D_MODEL, blk), lambda l, j: (l, 0, j)),
            pl.BlockSpec((1, 1, blk), lambda l, j: (l, 0, j)),
        ],
        out_specs=pl.BlockSpec((1, 8, blk), lambda l, j: (l, 0, j)),
        out_shape=jax.ShapeDtypeStruct((depth, 8, n_out), F32),
        compiler_params=_params(),
        name="ada",
    )(cond, w_ada, b_ada.reshape(depth, 1, n_out))


def _ffn_half_step(x, norm_w, shift, scale, gate, wi_ref, wo_ref):
    h = _rmsnorm(x, norm_w) * (1.0 + scale) + shift
    gu = _dot(h.astype(BF16), wi_ref[...])
    g = gu[:, :D_FF]
    u = gu[:, D_FF:]
    a = (g * (1.0 / (1.0 + jnp.exp(-g))) * u).astype(BF16)
    return x + (0.5 * gate) * _dot(a, wo_ref[...])


def _pre_kernel(x_ref, mod_ref, nw_ref, wi_ref, wo_ref, wqkv_ref, qnw_ref, kvnw_ref, wuqt_ref,
                wk_ref, wvt_ref, qcos_ref, qsin_ref, kcos_ref, ksin_ref,
                xo_ref, qt_ref, k_ref, vt_ref):
    x = x_ref[...]
    mod = mod_ref[...]
    nw = nw_ref[...]
    x1 = _ffn_half_step(x, nw[0:1], mod[0:1], mod[1:2], mod[2:3], wi_ref, wo_ref)
    xo_ref[...] = x1

    h = _rmsnorm(x1, nw[1:2]) * (1.0 + mod[4:5]) + mod[3:4]
    pj = _dot(h.astype(BF16), wqkv_ref[...])
    nq = _rmsnorm(pj[:, :Q_RANK], qnw_ref[...]).astype(BF16)
    nkv = _rmsnorm(pj[:, Q_RANK:Q_RANK + KV_RANK], kvnw_ref[...]).astype(BF16)
    kp = pj[:, Q_RANK + KV_RANK:]

    qt = _dot_nt(wuqt_ref[...], nq)
    qcos = qcos_ref[...]
    qsin = qsin_ref[...]
    for hd in range(HEADS):
        r = hd * HEAD_PAD
        qt_ref[r:r + NOPE, :] = (qt[r:r + NOPE] * Q_SCALE).astype(BF16)
        rot = qt[r + NOPE:r + QK_DIM] * qcos + qt[r + QK_DIM:r + HEAD_PAD] * qsin
        qt_ref[r + NOPE:r + QK_DIM, :] = rot.astype(BF16)
        qt_ref[r + QK_DIM:r + HEAD_PAD, :] = jnp.zeros((HEAD_PAD - QK_DIM, qt.shape[1]), BF16)

    krot = kp * kcos_ref[...] + pltpu.roll(kp, HEAD_PAD - ROPE, axis=1) * ksin_ref[...]
    k_ref[...] = _dot(jnp.concatenate([nkv, krot.astype(BF16)], axis=1), wk_ref[...]).astype(BF16)

    vt = _dot_nt(wvt_ref[...], nkv)
    row = lax.broadcasted_iota(jnp.int32, vt.shape, 0)
    vt_ref[...] = jnp.where(row % V_ROWS == V_DIM, 1.0, vt).astype(BF16)


def _pre(x, mods, norm_w, w, tabs, n_ctx_blocks):
    batch, tokens, _ = x.shape
    nb = tokens // TOKEN_BLOCK
    t = TOKEN_BLOCK

    def tok(b, i):
        return (b, i, 0)

    return pl.pallas_call(
        _pre_kernel,
        grid=(batch, nb),
        in_specs=[
            pl.BlockSpec((None, t, D_MODEL), tok),
            pl.BlockSpec((None, N_MOD, D_MODEL), lambda b, i: (jnp.where(i < n_ctx_blocks, batch, b), 0, 0)),
            _const_spec((3, D_MODEL)),
            _const_spec((D_MODEL, 2 * D_FF)),
            _const_spec((D_FF, D_MODEL)),
            _const_spec((D_MODEL, 512)),
            _const_spec((1, Q_RANK)),
            _const_spec((1, KV_RANK)),
            _const_spec((HEADS * HEAD_PAD, Q_RANK)),
            _const_spec((2 * KV_RANK, HEADS * HEAD_PAD)),
            _const_spec((HEADS * V_ROWS, KV_RANK)),
            pl.BlockSpec((ROPE, t), lambda b, i: (0, i)),
            pl.BlockSpec((ROPE, t), lambda b, i: (0, i)),
            pl.BlockSpec((t, HEAD_PAD), lambda b, i: (i, 0)),
            pl.BlockSpec((t, HEAD_PAD), lambda b, i: (i, 0)),
        ],
        out_specs=[
            pl.BlockSpec((None, t, D_MODEL), tok),
            pl.BlockSpec((None, HEADS * HEAD_PAD, t), lambda b, i: (b, 0, i)),
            pl.BlockSpec((None, t, HEADS * HEAD_PAD), tok),
            pl.BlockSpec((None, HEADS * V_ROWS, t), lambda b, i: (b, 0, i)),
        ],
        out_shape=[
            jax.ShapeDtypeStruct((batch, tokens, D_MODEL), F32),
            jax.ShapeDtypeStruct((batch, HEADS * HEAD_PAD, tokens), BF16),
            jax.ShapeDtypeStruct((batch, tokens, HEADS * HEAD_PAD), BF16),
            jax.ShapeDtypeStruct((batch, HEADS * V_ROWS, tokens), BF16),
        ],
        compiler_params=_params(),
        name="pre",
    )(x, mods, norm_w, w["wi1"], w["wo1"], w["wqkv"], w["qnw"], w["kvnw"], w["wuqt"], w["wk"], w["wvt"],
      tabs["qcos"], tabs["qsin"], tabs["kcos"], tabs["ksin"])


def _attn_kernel(*refs, n_keys, key_chunk, aliased):
    if aliased:
        refs = refs[1:]
    qt_ref, k_ref, vt_ref, o_ref, s_ref = refs
    n_chunks = n_keys // key_chunk
    tq = qt_ref.shape[1]
    outs = []
    for hd in range(ATTN_HEADS_PER_STEP):
        qt = qt_ref[hd * HEAD_PAD:(hd + 1) * HEAD_PAD, :]
        mx = None
        for c in range(n_chunks):
            rows = slice(c * key_chunk, (c + 1) * key_chunk)
            s = _dot(k_ref[rows, hd * HEAD_PAD:(hd + 1) * HEAD_PAD], qt)
            s_ref[hd, rows, :] = s
            cm = jnp.max(s, axis=0, keepdims=True)
            mx = cm if mx is None else jnp.maximum(mx, cm)
        acc = jnp.zeros((V_ROWS, tq), F32)
        for c in range(n_chunks):
            rows = slice(c * key_chunk, (c + 1) * key_chunk)
            p = jnp.exp2(s_ref[hd, rows, :] - mx).astype(BF16)
            acc = acc + _dot(vt_ref[hd * V_ROWS:(hd + 1) * V_ROWS, rows], p)
        outs.append(acc[:V_DIM] * (1.0 / acc[V_DIM:V_DIM + 1]))
    o_ref[...] = jnp.concatenate(outs, axis=0).T.astype(BF16)


def _attn(qt, k, vt, *, q_block_start, n_q_blocks, n_keys, att=None):
    batch, _, tokens = qt.shape
    hb = ATTN_HEADS_PER_STEP
    tq = TOKEN_BLOCK
    key_chunk = min(KEY_CHUNK, n_keys)
    assert n_keys % key_chunk == 0
    kernel = functools.partial(_attn_kernel, n_keys=n_keys, key_chunk=key_chunk, aliased=att is not None)
    in_specs = [
        pl.BlockSpec((None, hb * HEAD_PAD, tq), lambda b, g, i: (b, g, i + q_block_start)),
        pl.BlockSpec((None, n_keys, hb * HEAD_PAD), lambda b, g, i: (b, 0, g)),
        pl.BlockSpec((None, hb * V_ROWS, n_keys), lambda b, g, i: (b, g, 0)),
    ]
    args = [qt, k, vt]
    aliases = {}
    if att is not None:
        in_specs = [pl.BlockSpec(memory_space=pl.ANY)] + in_specs
        args = [att] + args
        aliases = {0: 0}
    return pl.pallas_call(
        kernel,
        grid=(batch, HEADS // hb, n_q_blocks),
        in_specs=in_specs,
        out_specs=pl.BlockSpec((None, tq, hb * V_DIM), lambda b, g, i: (b, i + q_block_start, g)),
        out_shape=jax.ShapeDtypeStruct((batch, tokens, HEADS * V_DIM), BF16),
        scratch_shapes=[pltpu.VMEM((hb, n_keys, tq), F32)],
        input_output_aliases=aliases,
        compiler_params=pltpu.CompilerParams(
            dimension_semantics=("arbitrary", "arbitrary", "arbitrary"), vmem_limit_bytes=VMEM_LIMIT_BYTES),
        name="attn_ctx" if att is not None else "attn",
    )(*args)


def _post_kernel(x_ref, xp_ref, xn_ref, att_ref, mod_ref, nw_ref, wcv_ref, cw_ref, wout_ref, wi_ref, wo_ref,
                 fnw_ref, o_ref, xe_ref, u_ref, *, block_start, n_ctx_blocks, n_blocks, final_norm):
    i = pl.program_id(1) + block_start
    t = TOKEN_BLOCK
    x = x_ref[...]
    mod = mod_ref[...]
    nw = nw_ref[...]

    xe_ref[0:HALO, :] = xp_ref[...]
    xe_ref[HALO:HALO + t, :] = x
    xe_ref[HALO + t:, :] = xn_ref[...]
    h = _rmsnorm(xe_ref[...], nw[1:2]) * (1.0 + mod[4:5]) + mod[3:4]
    pj = _dot(h.astype(BF16), wcv_ref[...])
    u = pj[:, CONV_DIM:2 * CONV_DIM] * pj[:, 2 * CONV_DIM:]
    left_ok = jnp.logical_and(i != 0, i != n_ctx_blocks)
    right_ok = jnp.logical_and(i != n_ctx_blocks - 1, i != n_blocks - 1)
    row = lax.broadcasted_iota(jnp.int32, (t + 2 * HALO, 1), 0)
    keep = jnp.logical_and(jnp.logical_or(row >= HALO, left_ok), jnp.logical_or(row < HALO + t, right_ok))
    u_ref[...] = jnp.where(keep, u, 0.0)
    cw = cw_ref[...]
    conv = (u_ref[HALO - 1:HALO - 1 + t, :] * cw[0:1] + u_ref[HALO:HALO + t, :] * cw[1:2]
            + u_ref[HALO + 1:HALO + 1 + t, :] * cw[2:3])
    conv = pj[HALO:HALO + t, :CONV_DIM] * conv

    mix = _dot(jnp.concatenate([att_ref[...], conv.astype(BF16)], axis=1), wout_ref[...])
    x1 = x + mod[5:6] * mix
    x2 = _ffn_half_step(x1, nw[2:3], mod[6:7], mod[7:8], mod[8:9], wi_ref, wo_ref)
    if final_norm:
        x2 = _rmsnorm(x2, fnw_ref[...])
    o_ref[...] = x2


def _post(x, att, mods, norm_w, w, final_norm_w, *, n_ctx_blocks, block_start, out_tokens, final_norm):
    batch, tokens, _ = x.shape
    t = TOKEN_BLOCK
    nb = tokens // t
    halo_per_block = t // HALO
    n_halo = tokens // HALO
    kernel = functools.partial(_post_kernel, block_start=block_start, n_ctx_blocks=n_ctx_blocks, n_blocks=nb,
                               final_norm=final_norm)
    out_block_shift = block_start if out_tokens == tokens else 0
    return pl.pallas_call(
        kernel,
        grid=(batch, nb - block_start),
        in_specs=[
            pl.BlockSpec((None, t, D_MODEL), lambda b, i: (b, i + block_start, 0)),
            pl.BlockSpec((None, HALO, D_MODEL),
                         lambda b, i: (b, jnp.maximum((i + block_start) * halo_per_block - 1, 0), 0)),
            pl.BlockSpec((None, HALO, D_MODEL),
                         lambda b, i: (b, jnp.minimum((i + block_start + 1) * halo_per_block, n_halo - 1), 0)),
            pl.BlockSpec((None, t, HEADS * V_DIM), lambda b, i: (b, i + block_start, 0)),
            pl.BlockSpec((None, N_MOD, D_MODEL),
                         lambda b, i: (jnp.where(i + block_start < n_ctx_blocks, batch, b), 0, 0)),
            _const_spec((3, D_MODEL)),
            _const_spec((D_MODEL, 3 * CONV_DIM)),
            _const_spec((3, CONV_DIM)),
            _const_spec((HEADS * V_DIM + CONV_DIM, D_MODEL)),
            _const_spec((D_MODEL, 2 * D_FF)),
            _const_spec((D_FF, D_MODEL)),
            _const_spec((1, D_MODEL)),
        ],
        out_specs=pl.BlockSpec((None, t, D_MODEL), lambda b, i: (b, i + out_block_shift, 0)),
        out_shape=jax.ShapeDtypeStruct((batch, out_tokens, D_MODEL), F32),
        scratch_shapes=[pltpu.VMEM((t + 2 * HALO, D_MODEL), F32), pltpu.VMEM((t + 2 * HALO, CONV_DIM), F32)],
        compiler_params=_params(),
        name="post",
    )(x, x, x, att, mods, norm_w, w["wcv"], w["cw"], w["wout"], w["wi2"], w["wo2"], final_norm_w)


def _layer_weights(l, w_ffn1_in, w_ffn1_out, w_ffn2_in, w_ffn2_out, w_in, q_norm_w, kv_norm_w, w_uq, w_ukv,
                   conv_w, w_out):
    partner = jnp.array(_ROPE_PARTNER)
    w_in_l = w_in[l]
    kv0 = Q_RANK + KV_RANK
    kpe = w_in_l[:, kv0:kv0 + ROPE]
    wqkv = jnp.concatenate(
        [w_in_l[:, :kv0], kpe, kpe[:, partner], jnp.zeros((D_MODEL, HEAD_PAD - 2 * ROPE), F32)], axis=1)

    uq = w_uq[l].reshape(Q_RANK, HEADS, QK_DIM)
    uq = jnp.concatenate([uq, uq[:, :, NOPE:][:, :, partner]], axis=2)
    wuqt = uq.reshape(Q_RANK, HEADS * HEAD_PAD).T

    ukv = w_ukv[l].reshape(KV_RANK, HEADS, NOPE + V_DIM)
    wk_nope = jnp.concatenate([ukv[:, :, :NOPE], jnp.zeros((KV_RANK, HEADS, HEAD_PAD - NOPE), F32)], axis=2)
    place = jnp.zeros((KV_RANK, HEADS, HEAD_PAD), F32)
    place = place.at[jnp.arange(ROPE), :, NOPE + jnp.arange(ROPE)].set(1.0)
    wk = jnp.concatenate([wk_nope, place], axis=0).reshape(2 * KV_RANK, HEADS * HEAD_PAD)

    wv = jnp.concatenate([ukv[:, :, NOPE:], jnp.zeros((KV_RANK, HEADS, V_ROWS - V_DIM), F32)], axis=2)
    wvt = wv.reshape(KV_RANK, HEADS * V_ROWS).T

    return {
        "wi1": w_ffn1_in[l].astype(BF16), "wo1": w_ffn1_out[l].astype(BF16),
        "wi2": w_ffn2_in[l].astype(BF16), "wo2": w_ffn2_out[l].astype(BF16),
        "wqkv": wqkv.astype(BF16), "wcv": w_in_l[:, kv0 + ROPE:].astype(BF16),
        "qnw": q_norm_w[l].reshape(1, Q_RANK), "kvnw": kv_norm_w[l].reshape(1, KV_RANK),
        "wuqt": wuqt.astype(BF16), "wk": wk.astype(BF16), "wvt": wvt.astype(BF16),
        "cw": conv_w[l], "wout": w_out[l].astype(BF16),
    }


def _rope_tables(n_ctx, n_lat):
    t = jnp.arange(n_lat)
    row = (t // GRID_W).astype(F32)
    col = (t % GRID_W).astype(F32)
    d_axis = ROPE // 2
    inv = ROPE_BASE ** (-jnp.arange(0, d_axis, 2, dtype=F32) / d_axis)
    ar = row[:, None] * inv
    ac = col[:, None] * inv
    cr, sr, cc, sc = jnp.cos(ar), jnp.sin(ar), jnp.cos(ac), jnp.sin(ac)
    cos = jnp.concatenate([cr, cr, cc, cc], axis=1)
    sin = jnp.concatenate([-sr, sr, -sc, sc], axis=1)
    cos = jnp.concatenate([jnp.ones((n_ctx, ROPE), F32), cos], axis=0)
    sin = jnp.concatenate([jnp.zeros((n_ctx, ROPE), F32), sin], axis=0)
    pad = jnp.zeros((n_ctx + n_lat, HEAD_PAD - ROPE), F32)
    return {
        "qcos": (cos * Q_SCALE).T, "qsin": (sin * Q_SCALE).T,
        "kcos": jnp.concatenate([cos, pad], axis=1), "ksin": jnp.concatenate([sin, pad], axis=1),
    }


def kernel(x, c, ctx, c_ctx, w_ada, b_ada, norm_w, w_ffn1_in, w_ffn1_out, w_ffn2_in, w_ffn2_out, w_in, q_norm_w,
           kv_norm_w, w_uq, w_ukv, conv_w, w_out, final_norm_w):
    batch, n_lat, _ = x.shape
    n_ctx = ctx.shape[1]
    depth = w_ada.shape[0]
    assert n_ctx % TOKEN_BLOCK == 0 and n_lat % TOKEN_BLOCK == 0 and batch + 1 <= 8
    tokens = n_ctx + n_lat
    n_ctx_blocks = n_ctx // TOKEN_BLOCK
    n_blocks = tokens // TOKEN_BLOCK

    cond = jnp.concatenate([c, c_ctx[None, :], jnp.zeros((8 - batch - 1, D_MODEL), F32)], axis=0)
    mods = _ada(cond, w_ada, b_ada).reshape(depth, 8, N_MOD, D_MODEL)
    tabs = _rope_tables(n_ctx, n_lat)
    xs = jnp.concatenate([ctx, x], axis=1)
    fnw = final_norm_w.reshape(1, D_MODEL)

    for l in range(depth):
        last = l == depth - 1
        w = _layer_weights(l, w_ffn1_in, w_ffn1_out, w_ffn2_in, w_ffn2_out, w_in, q_norm_w, kv_norm_w, w_uq,
                           w_ukv, conv_w, w_out)
        xs, qt, k, vt = _pre(xs, mods[l], norm_w[l], w, tabs, n_ctx_blocks)
        att = _attn(qt, k, vt, q_block_start=n_ctx_blocks, n_q_blocks=n_blocks - n_ctx_blocks, n_keys=tokens)
        if not last:
            att = _attn(qt, k, vt, q_block_start=0, n_q_blocks=n_ctx_blocks, n_keys=n_ctx, att=att)
        xs = _post(xs, att, mods[l], norm_w[l], w, fnw, n_ctx_blocks=n_ctx_blocks,
                   block_start=n_ctx_blocks if last else 0, out_tokens=n_lat if last else tokens,
                   final_norm=last)
    return xs
```

```python
import functools
import math

import jax
import jax.numpy as jnp
from jax import lax
from jax.experimental import pallas as pl
from jax.experimental.pallas import tpu as pltpu

F32 = jnp.float32
BF16 = jnp.bfloat16

D_MODEL = 1024
HEADS = 8
NOPE = 64
ROPE = 32
QK_DIM = NOPE + ROPE
V_DIM = 64
Q_RANK = 256
KV_RANK = 128
CONV_DIM = 512
D_FF = 2816
N_MOD = 9
GRID_W = 64
ROPE_BASE = 10000.0
EPS = 1e-6

HEAD_PAD = 128
V_ROWS = 80
TOKEN_BLOCK = 256
KEY_CHUNK = 768
MAX_ROWS = 32
HALO = 8
ATTN_HEADS_PER_STEP = 2
VMEM_LIMIT_BYTES = 56 * 1024 * 1024

Q_SCALE = math.log2(math.e) / math.sqrt(QK_DIM)

_ROPE_PARTNER = tuple(range(8, 16)) + tuple(range(0, 8)) + tuple(range(24, 32)) + tuple(range(16, 24))


def _dot(a, b):
    return jnp.dot(a, b, preferred_element_type=F32)


def _dot_nt(a, b):
    return lax.dot_general(a, b, (((1,), (1,)), ((), ())), preferred_element_type=F32)


def _rmsnorm(x, w):
    r = lax.rsqrt(jnp.mean(x * x, axis=-1, keepdims=True) + EPS)
    return (x * r) * w


def _const_spec(shape):
    return pl.BlockSpec(shape, lambda *_: (0,) * len(shape), pipeline_mode=pl.Buffered(1))


def _params():
    return pltpu.CompilerParams(
        dimension_semantics=("arbitrary", "arbitrary"), vmem_limit_bytes=VMEM_LIMIT_BYTES)


def _ada_kernel(c_ref, w_ref, b_ref, o_ref):
    c = c_ref[...]
    act = (c * (1.0 / (1.0 + jnp.exp(-c)))).astype(BF16)
    o_ref[0] = _dot(act, w_ref[0].astype(BF16)) + b_ref[0]


def _ada(cond, w_ada, b_ada):
    depth = w_ada.shape[0]
    n_out = w_ada.shape[2]
    blk = 1024
    return pl.pallas_call(
        _ada_kernel,
        grid=(depth, n_out // blk),
        in_specs=[
            pl.BlockSpec((8, D_MODEL), lambda l, j: (0, 0)),
            pl.BlockSpec((1, D_MODEL, blk), lambda l, j: (l, 0, j)),
            pl.BlockSpec((1, 1, blk), lambda l, j: (l, 0, j)),
        ],
        out_specs=pl.BlockSpec((1, 8, blk), lambda l, j: (l, 0, j)),
        out_shape=jax.ShapeDtypeStruct((depth, 8, n_out), F32),
        compiler_params=_params(),
        name="ada",
    )(cond, w_ada, b_ada.reshape(depth, 1, n_out))


def _ffn_half_step(x, norm_w, shift, scale, gate, wi_ref, wo_ref):
    h = _rmsnorm(x, norm_w) * (1.0 + scale) + shift
    gu = _dot(h.astype(BF16), wi_ref[...])
    g = gu[:, :D_FF]
    u = gu[:, D_FF:]
    a = (g * (1.0 / (1.0 + jnp.exp(-g))) * u).astype(BF16)
    return x + (0.5 * gate) * _dot(a, wo_ref[...])


def _pre_kernel(x_ref, mod_ref, nw_ref, wi_ref, wo_ref, wqkv_ref, qnw_ref, kvnw_ref, wuqt_ref,
                wk_ref, wvt_ref, qcos_ref, qsin_ref, kcos_ref, ksin_ref,
                xo_ref, qt_ref, k_ref, vt_ref):
    x = x_ref[...]
    mod = mod_ref[...]
    nw = nw_ref[...]
    x1 = _ffn_half_step(x, nw[0:1], mod[0:1], mod[1:2], mod[2:3], wi_ref, wo_ref)
    xo_ref[...] = x1

    h = _rmsnorm(x1, nw[1:2]) * (1.0 + mod[4:5]) + mod[3:4]
    pj = _dot(h.astype(BF16), wqkv_ref[...])
    nq = _rmsnorm(pj[:, :Q_RANK], qnw_ref[...]).astype(BF16)
    nkv = _rmsnorm(pj[:, Q_RANK:Q_RANK + KV_RANK], kvnw_ref[...]).astype(BF16)
    kp = pj[:, Q_RANK + KV_RANK:]

    qt = _dot_nt(wuqt_ref[...], nq)
    qcos = qcos_ref[...]
    qsin = qsin_ref[...]
    for hd in range(HEADS):
        r = hd * HEAD_PAD
        qt_ref[r:r + NOPE, :] = (qt[r:r + NOPE] * Q_SCALE).astype(BF16)
        rot = qt[r + NOPE:r + QK_DIM] * qcos + qt[r + QK_DIM:r + HEAD_PAD] * qsin
        qt_ref[r + NOPE:r + QK_DIM, :] = rot.astype(BF16)
        qt_ref[r + QK_DIM:r + HEAD_PAD, :] = jnp.zeros((HEAD_PAD - QK_DIM, qt.shape[1]), BF16)

    krot = kp * kcos_ref[...] + pltpu.roll(kp, HEAD_PAD - ROPE, axis=1) * ksin_ref[...]
    k_ref[...] = _dot(jnp.concatenate([nkv, krot.astype(BF16)], axis=1), wk_ref[...]).astype(BF16)

    vt = _dot_nt(wvt_ref[...], nkv)
    row = lax.broadcasted_iota(jnp.int32, vt.shape, 0)
    vt_ref[...] = jnp.where(row % V_ROWS == V_DIM, 1.0, vt).astype(BF16)


def _pre(x, mods, norm_w, w, tabs, n_ctx_blocks):
    batch, tokens, _ = x.shape
    nb = tokens // TOKEN_BLOCK
    t = TOKEN_BLOCK

    def tok(b, i):
        return (b, i, 0)

    return pl.pallas_call(
        _pre_kernel,
        grid=(batch, nb),
        in_specs=[
            pl.BlockSpec((None, t, D_MODEL), tok),
            pl.BlockSpec((None, N_MOD, D_MODEL), lambda b, i: (jnp.where(i < n_ctx_blocks, batch, b), 0, 0)),
            _const_spec((3, D_MODEL)),
            _const_spec((D_MODEL, 2 * D_FF)),
            _const_spec((D_FF, D_MODEL)),
            _const_spec((D_MODEL, 512)),
            _const_spec((1, Q_RANK)),
            _const_spec((1, KV_RANK)),
            _const_spec((HEADS * HEAD_PAD, Q_RANK)),
            _const_spec((2 * KV_RANK, HEADS * HEAD_PAD)),
            _const_spec((HEADS * V_ROWS, KV_RANK)),
            pl.BlockSpec((ROPE, t), lambda b, i: (0, i)),
            pl.BlockSpec((ROPE, t), lambda b, i: (0, i)),
            pl.BlockSpec((t, HEAD_PAD), lambda b, i: (i, 0)),
            pl.BlockSpec((t, HEAD_PAD), lambda b, i: (i, 0)),
        ],
        out_specs=[
            pl.BlockSpec((None, t, D_MODEL), tok),
            pl.BlockSpec((None, HEADS * HEAD_PAD, t), lambda b, i: (b, 0, i)),
            pl.BlockSpec((None, t, HEADS * HEAD_PAD), tok),
            pl.BlockSpec((None, HEADS * V_ROWS, t), lambda b, i: (b, 0, i)),
        ],
        out_shape=[
            jax.ShapeDtypeStruct((batch, tokens, D_MODEL), F32),
            jax.ShapeDtypeStruct((batch, HEADS * HEAD_PAD, tokens), BF16),
            jax.ShapeDtypeStruct((batch, tokens, HEADS * HEAD_PAD), BF16),
            jax.ShapeDtypeStruct((batch, HEADS * V_ROWS, tokens), BF16),
        ],
        compiler_params=_params(),
        name="pre",
    )(x, mods, norm_w, w["wi1"], w["wo1"], w["wqkv"], w["qnw"], w["kvnw"], w["wuqt"], w["wk"], w["wvt"],
      tabs["qcos"], tabs["qsin"], tabs["kcos"], tabs["ksin"])


def _column_max(s, mx):
    parts = [s[r:r + MAX_ROWS] for r in range(0, s.shape[0], MAX_ROWS)]
    if mx is not None:
        parts.append(mx)
    while len(parts) > 1:
        nxt = [jnp.maximum(parts[j], parts[j + 1]) for j in range(0, len(parts) - 1, 2)]
        if len(parts) % 2:
            nxt.append(parts[-1])
        parts = nxt
    return parts[0]


def _scores(k_ref, qt, s_ref, m_ref, n_chunks, key_chunk):
    mx = None
    for c in range(n_chunks):
        rows = slice(c * key_chunk, (c + 1) * key_chunk)
        s = _dot(k_ref[rows, :], qt)
        s_ref[rows, :] = s
        mx = _column_max(s, mx)
    m_ref[...] = jnp.max(mx, axis=0, keepdims=True)


def _attn_kernel(qt0_ref, k0_ref, qtn_ref, kn_ref, vt_ref, o_ref, s_ref, m_ref, *, n_keys, key_chunk):
    n_chunks = n_keys // key_chunk
    tq = qtn_ref.shape[1]
    first = jnp.logical_and(jnp.logical_and(pl.program_id(0) == 0, pl.program_id(1) == 0),
                            pl.program_id(2) == 0)

    @pl.when(first)
    def _():
        _scores(k0_ref, qt0_ref[...], s_ref, m_ref, n_chunks, key_chunk)

    m = m_ref[...]
    qtn = qtn_ref[...]
    acc = jnp.zeros((V_ROWS, tq), F32)
    mx = None
    for c in range(n_chunks):
        rows = slice(c * key_chunk, (c + 1) * key_chunk)
        p = jnp.exp2(s_ref[rows, :] - m).astype(BF16)
        acc = acc + _dot(vt_ref[:, rows], p)
        s = _dot(kn_ref[rows, :], qtn)
        s_ref[rows, :] = s
        mx = _column_max(s, mx)
    m_ref[...] = jnp.max(mx, axis=0, keepdims=True)
    o_ref[...] = (acc[:V_DIM] * (1.0 / acc[V_DIM:V_DIM + 1])).astype(BF16)


def _attn(qt, k, vt, *, q_block_start, n_q_blocks):
    batch, _, tokens = qt.shape
    tq = TOKEN_BLOCK
    assert tokens % KEY_CHUNK == 0

    def nxt(b, h, i):
        wrap_i = i == n_q_blocks - 1
        wrap_h = jnp.logical_and(wrap_i, h == HEADS - 1)
        last = jnp.logical_and(wrap_h, b == batch - 1)
        i2 = jnp.where(wrap_i, 0, i + 1)
        h2 = jnp.where(wrap_i, jnp.where(h == HEADS - 1, 0, h + 1), h)
        b2 = jnp.where(wrap_h, b + 1, b)
        return jnp.where(last, b, b2), jnp.where(last, h, h2), jnp.where(last, i, i2)

    def qt_next(b, h, i):
        b2, h2, i2 = nxt(b, h, i)
        return (b2, h2, i2 + q_block_start)

    def k_next(b, h, i):
        b2, h2, _ = nxt(b, h, i)
        return (b2, 0, h2)

    kernel = functools.partial(_attn_kernel, n_keys=tokens, key_chunk=KEY_CHUNK)
    return pl.pallas_call(
        kernel,
        grid=(batch, HEADS, n_q_blocks),
        in_specs=[
            pl.BlockSpec((None, HEAD_PAD, tq), lambda b, h, i: (0, 0, q_block_start),
                         pipeline_mode=pl.Buffered(1)),
            pl.BlockSpec((None, tokens, HEAD_PAD), lambda b, h, i: (0, 0, 0), pipeline_mode=pl.Buffered(1)),
            pl.BlockSpec((None, HEAD_PAD, tq), qt_next),
            pl.BlockSpec((None, tokens, HEAD_PAD), k_next),
            pl.BlockSpec((None, V_ROWS, tokens), lambda b, h, i: (b, h, 0)),
        ],
        out_specs=pl.BlockSpec((None, V_DIM, tq), lambda b, h, i: (b, h, i + q_block_start)),
        out_shape=jax.ShapeDtypeStruct((batch, HEADS * V_DIM, tokens), BF16),
        scratch_shapes=[pltpu.VMEM((tokens, tq), F32), pltpu.VMEM((1, tq), F32)],
        compiler_params=pltpu.CompilerParams(
            dimension_semantics=("arbitrary", "arbitrary", "arbitrary"), vmem_limit_bytes=VMEM_LIMIT_BYTES),
        name="attn",
    )(qt, k, qt, k, vt)


def _attn_ctx_kernel(att_ref, qt_ref, k_ref, vt_ref, o_ref):
    del att_ref
    for hd in range(ATTN_HEADS_PER_STEP):
        s = _dot(k_ref[:, hd * HEAD_PAD:(hd + 1) * HEAD_PAD], qt_ref[hd * HEAD_PAD:(hd + 1) * HEAD_PAD, :])
        p = jnp.exp2(s - jnp.max(s, axis=0, keepdims=True)).astype(BF16)
        acc = _dot(vt_ref[hd * V_ROWS:(hd + 1) * V_ROWS, :], p)
        o_ref[hd * V_DIM:(hd + 1) * V_DIM, :] = (acc[:V_DIM] * (1.0 / acc[V_DIM:V_DIM + 1])).astype(BF16)


def _attn_ctx(att, qt, k, vt, *, n_ctx):
    batch = qt.shape[0]
    hb = ATTN_HEADS_PER_STEP
    return pl.pallas_call(
        _attn_ctx_kernel,
        grid=(batch, HEADS // hb),
        in_specs=[
            pl.BlockSpec(memory_space=pl.ANY),
            pl.BlockSpec((None, hb * HEAD_PAD, n_ctx), lambda b, g: (b, g, 0)),
            pl.BlockSpec((None, n_ctx, hb * HEAD_PAD), lambda b, g: (b, 0, g)),
            pl.BlockSpec((None, hb * V_ROWS, n_ctx), lambda b, g: (b, g, 0)),
        ],
        out_specs=pl.BlockSpec((None, hb * V_DIM, n_ctx), lambda b, g: (b, g, 0)),
        out_shape=jax.ShapeDtypeStruct(att.shape, BF16),
        input_output_aliases={0: 0},
        compiler_params=_params(),
        name="attn_ctx",
    )(att, qt, k, vt)


def _post_kernel(x_ref, xp_ref, xn_ref, att_ref, mod_ref, nw_ref, wcv_ref, cw_ref, wout_ref, wi_ref, wo_ref,
                 fnw_ref, o_ref, xe_ref, u_ref, *, block_start, n_ctx_blocks, n_blocks, final_norm):
    i = pl.program_id(1) + block_start
    t = TOKEN_BLOCK
    x = x_ref[...]
    mod = mod_ref[...]
    nw = nw_ref[...]

    xe_ref[0:HALO, :] = xp_ref[...]
    xe_ref[HALO:HALO + t, :] = x
    xe_ref[HALO + t:, :] = xn_ref[...]
    h = _rmsnorm(xe_ref[...], nw[1:2]) * (1.0 + mod[4:5]) + mod[3:4]
    pj = _dot(h.astype(BF16), wcv_ref[...])
    u = pj[:, CONV_DIM:2 * CONV_DIM] * pj[:, 2 * CONV_DIM:]
    left_ok = jnp.logical_and(i != 0, i != n_ctx_blocks)
    right_ok = jnp.logical_and(i != n_ctx_blocks - 1, i != n_blocks - 1)
    row = lax.broadcasted_iota(jnp.int32, (t + 2 * HALO, 1), 0)
    keep = jnp.logical_and(jnp.logical_or(row >= HALO, left_ok), jnp.logical_or(row < HALO + t, right_ok))
    u_ref[...] = jnp.where(keep, u, 0.0)
    cw = cw_ref[...]
    conv = (u_ref[HALO - 1:HALO - 1 + t, :] * cw[0:1] + u_ref[HALO:HALO + t, :] * cw[1:2]
            + u_ref[HALO + 1:HALO + 1 + t, :] * cw[2:3])
    conv = pj[HALO:HALO + t, :CONV_DIM] * conv

    mix = _dot(jnp.concatenate([att_ref[...].T, conv.astype(BF16)], axis=1), wout_ref[...])
    x1 = x + mod[5:6] * mix
    x2 = _ffn_half_step(x1, nw[2:3], mod[6:7], mod[7:8], mod[8:9], wi_ref, wo_ref)
    if final_norm:
        x2 = _rmsnorm(x2, fnw_ref[...])
    o_ref[...] = x2


def _post(x, att, mods, norm_w, w, final_norm_w, *, n_ctx_blocks, block_start, out_tokens, final_norm):
    batch, tokens, _ = x.shape
    t = TOKEN_BLOCK
    nb = tokens // t
    halo_per_block = t // HALO
    n_halo = tokens // HALO
    kernel = functools.partial(_post_kernel, block_start=block_start, n_ctx_blocks=n_ctx_blocks, n_blocks=nb,
                               final_norm=final_norm)
    out_block_shift = block_start if out_tokens == tokens else 0
    return pl.pallas_call(
        kernel,
        grid=(batch, nb - block_start),
        in_specs=[
            pl.BlockSpec((None, t, D_MODEL), lambda b, i: (b, i + block_start, 0)),
            pl.BlockSpec((None, HALO, D_MODEL),
                         lambda b, i: (b, jnp.maximum((i + block_start) * halo_per_block - 1, 0), 0)),
            pl.BlockSpec((None, HALO, D_MODEL),
                         lambda b, i: (b, jnp.minimum((i + block_start + 1) * halo_per_block, n_halo - 1), 0)),
            pl.BlockSpec((None, HEADS * V_DIM, t), lambda b, i: (b, 0, i + block_start)),
            pl.BlockSpec((None, N_MOD, D_MODEL),
                         lambda b, i: (jnp.where(i + block_start < n_ctx_blocks, batch, b), 0, 0)),
            _const_spec((3, D_MODEL)),
            _const_spec((D_MODEL, 3 * CONV_DIM)),
            _const_spec((3, CONV_DIM)),
            _const_spec((HEADS * V_DIM + CONV_DIM, D_MODEL)),
            _const_spec((D_MODEL, 2 * D_FF)),
            _const_spec((D_FF, D_MODEL)),
            _const_spec((1, D_MODEL)),
        ],
        out_specs=pl.BlockSpec((None, t, D_MODEL), lambda b, i: (b, i + out_block_shift, 0)),
        out_shape=jax.ShapeDtypeStruct((batch, out_tokens, D_MODEL), F32),
        scratch_shapes=[pltpu.VMEM((t + 2 * HALO, D_MODEL), F32), pltpu.VMEM((t + 2 * HALO, CONV_DIM), F32)],
        compiler_params=_params(),
        name="post",
    )(x, x, x, att, mods, norm_w, w["wcv"], w["cw"], w["wout"], w["wi2"], w["wo2"], final_norm_w)


def _layer_weights(l, w_ffn1_in, w_ffn1_out, w_ffn2_in, w_ffn2_out, w_in, q_norm_w, kv_norm_w, w_uq, w_ukv,
                   conv_w, w_out):
    partner = jnp.array(_ROPE_PARTNER)
    w_in_l = w_in[l]
    kv0 = Q_RANK + KV_RANK
    kpe = w_in_l[:, kv0:kv0 + ROPE]
    wqkv = jnp.concatenate(
        [w_in_l[:, :kv0], kpe, kpe[:, partner], jnp.zeros((D_MODEL, HEAD_PAD - 2 * ROPE), F32)], axis=1)

    uq = w_uq[l].reshape(Q_RANK, HEADS, QK_DIM)
    uq = jnp.concatenate([uq, uq[:, :, NOPE:][:, :, partner]], axis=2)
    wuqt = uq.reshape(Q_RANK, HEADS * HEAD_PAD).T

    ukv = w_ukv[l].reshape(KV_RANK, HEADS, NOPE + V_DIM)
    wk_nope = jnp.concatenate([ukv[:, :, :NOPE], jnp.zeros((KV_RANK, HEADS, HEAD_PAD - NOPE), F32)], axis=2)
    place = jnp.zeros((KV_RANK, HEADS, HEAD_PAD), F32)
    place = place.at[jnp.arange(ROPE), :, NOPE + jnp.arange(ROPE)].set(1.0)
    wk = jnp.concatenate([wk_nope, place], axis=0).reshape(2 * KV_RANK, HEADS * HEAD_PAD)

    wv = jnp.concatenate([ukv[:, :, NOPE:], jnp.zeros((KV_RANK, HEADS, V_ROWS - V_DIM), F32)], axis=2)
    wvt = wv.reshape(KV_RANK, HEADS * V_ROWS).T

    return {
        "wi1": w_ffn1_in[l].astype(BF16), "wo1": w_ffn1_out[l].astype(BF16),
        "wi2": w_ffn2_in[l].astype(BF16), "wo2": w_ffn2_out[l].astype(BF16),
        "wqkv": wqkv.astype(BF16), "wcv": w_in_l[:, kv0 + ROPE:].astype(BF16),
        "qnw": q_norm_w[l].reshape(1, Q_RANK), "kvnw": kv_norm_w[l].reshape(1, KV_RANK),
        "wuqt": wuqt.astype(BF16), "wk": wk.astype(BF16), "wvt": wvt.astype(BF16),
        "cw": conv_w[l], "wout": w_out[l].astype(BF16),
    }


def _rope_tables(n_ctx, n_lat):
    t = jnp.arange(n_lat)
    row = (t // GRID_W).astype(F32)
    col = (t % GRID_W).astype(F32)
    d_axis = ROPE // 2
    inv = ROPE_BASE ** (-jnp.arange(0, d_axis, 2, dtype=F32) / d_axis)
    ar = row[:, None] * inv
    ac = col[:, None] * inv
    cr, sr, cc, sc = jnp.cos(ar), jnp.sin(ar), jnp.cos(ac), jnp.sin(ac)
    cos = jnp.concatenate([cr, cr, cc, cc], axis=1)
    sin = jnp.concatenate([-sr, sr, -sc, sc], axis=1)
    cos = jnp.concatenate([jnp.ones((n_ctx, ROPE), F32), cos], axis=0)
    sin = jnp.concatenate([jnp.zeros((n_ctx, ROPE), F32), sin], axis=0)
    pad = jnp.zeros((n_ctx + n_lat, HEAD_PAD - ROPE), F32)
    return {
        "qcos": (cos * Q_SCALE).T, "qsin": (sin * Q_SCALE).T,
        "kcos": jnp.concatenate([cos, pad], axis=1), "ksin": jnp.concatenate([sin, pad], axis=1),
    }


def kernel(x, c, ctx, c_ctx, w_ada, b_ada, norm_w, w_ffn1_in, w_ffn1_out, w_ffn2_in, w_ffn2_out, w_in, q_norm_w,
           kv_norm_w, w_uq, w_ukv, conv_w, w_out, final_norm_w):
    batch, n_lat, _ = x.shape
    n_ctx = ctx.shape[1]
    depth = w_ada.shape[0]
    assert n_ctx % TOKEN_BLOCK == 0 and n_lat % TOKEN_BLOCK == 0 and batch + 1 <= 8
    tokens = n_ctx + n_lat
    n_ctx_blocks = n_ctx // TOKEN_BLOCK
    n_blocks = tokens // TOKEN_BLOCK

    cond = jnp.concatenate([c, c_ctx[None, :], jnp.zeros((8 - batch - 1, D_MODEL), F32)], axis=0)
    mods = _ada(cond, w_ada, b_ada).reshape(depth, 8, N_MOD, D_MODEL)
    tabs = _rope_tables(n_ctx, n_lat)
    xs = jnp.concatenate([ctx, x], axis=1)
    fnw = final_norm_w.reshape(1, D_MODEL)

    for l in range(depth):
        last = l == depth - 1
        w = _layer_weights(l, w_ffn1_in, w_ffn1_out, w_ffn2_in, w_ffn2_out, w_in, q_norm_w, kv_norm_w, w_uq,
                           w_ukv, conv_w, w_out)
        xs, qt, k, vt = _pre(xs, mods[l], norm_w[l], w, tabs, n_ctx_blocks)
        att = _attn(qt, k, vt, q_block_start=n_ctx_blocks, n_q_blocks=n_blocks - n_ctx_blocks)
        if not last:
            att = _attn_ctx(att, qt, k, vt, n_ctx=n_ctx)
        xs = _post(xs, att, mods[l], norm_w[l], w, fnw, n_ctx_blocks=n_ctx_blocks,
                   block_start=n_ctx_blocks if last else 0, out_tokens=n_lat if last else tokens,
                   final_norm=last)
    return xs
```

```python
import functools
import math

import jax
import jax.numpy as jnp
from jax import lax
from jax.experimental import pallas as pl
from jax.experimental.pallas import tpu as pltpu

F32 = jnp.float32
BF16 = jnp.bfloat16

D_MODEL = 1024
HEADS = 8
NOPE = 64
ROPE = 32
QK_DIM = NOPE + ROPE
V_DIM = 64
Q_RANK = 256
KV_RANK = 128
CONV_DIM = 512
D_FF = 2816
N_MOD = 9
GRID_W = 64
ROPE_BASE = 10000.0
EPS = 1e-6

HEAD_PAD = 128
V_ROWS = 80
TOKEN_BLOCK = 256
QUERY_BLOCK = 512
KEY_CHUNK = 768
MAX_ROWS = 32
HALO = 8
ATTN_HEADS_PER_STEP = 2
VMEM_LIMIT_BYTES = 56 * 1024 * 1024

Q_SCALE = math.log2(math.e) / math.sqrt(QK_DIM)

_ROPE_PARTNER = tuple(range(8, 16)) + tuple(range(0, 8)) + tuple(range(24, 32)) + tuple(range(16, 24))


def _dot(a, b):
    return jnp.dot(a, b, preferred_element_type=F32)


def _dot_nt(a, b):
    return lax.dot_general(a, b, (((1,), (1,)), ((), ())), preferred_element_type=F32)


def _rmsnorm(x, w):
    r = lax.rsqrt(jnp.mean(x * x, axis=-1, keepdims=True) + EPS)
    return (x * r) * w


def _const_spec(shape):
    return pl.BlockSpec(shape, lambda *_: (0,) * len(shape), pipeline_mode=pl.Buffered(1))


def _params():
    return pltpu.CompilerParams(
        dimension_semantics=("arbitrary", "arbitrary"), vmem_limit_bytes=VMEM_LIMIT_BYTES)


def _ada_kernel(c_ref, w_ref, b_ref, o_ref):
    c = c_ref[...]
    act = (c * (1.0 / (1.0 + jnp.exp(-c)))).astype(BF16)
    o_ref[0] = _dot(act, w_ref[0].astype(BF16)) + b_ref[0]


def _ada(cond, w_ada, b_ada):
    depth = w_ada.shape[0]
    n_out = w_ada.shape[2]
    blk = 1024
    return pl.pallas_call(
        _ada_kernel,
        grid=(depth, n_out // blk),
        in_specs=[
            pl.BlockSpec((8, D_MODEL), lambda l, j: (0, 0)),
            pl.BlockSpec((1, D_MODEL, blk), lambda l, j: (l, 0, j)),
            pl.BlockSpec((1, 1, blk), lambda l, j: (l, 0, j)),
        ],
        out_specs=pl.BlockSpec((1, 8, blk), lambda l, j: (l, 0, j)),
        out_shape=jax.ShapeDtypeStruct((depth, 8, n_out), F32),
        compiler_params=_params(),
        name="ada",
    )(cond, w_ada, b_ada.reshape(depth, 1, n_out))


def _ffn_half_step(x, norm_w, shift, scale, gate, wi_ref, wo_ref):
    h = _rmsnorm(x, norm_w) * (1.0 + scale) + shift
    gu = _dot(h.astype(BF16), wi_ref[...])
    g = gu[:, :D_FF]
    u = gu[:, D_FF:]
    a = (g * (1.0 / (1.0 + jnp.exp(-g))) * u).astype(BF16)
    return x + (0.5 * gate) * _dot(a, wo_ref[...])


def _pre_kernel(x_ref, mod_ref, nw_ref, wi_ref, wo_ref, wqkv_ref, qnw_ref, kvnw_ref, wuqt_ref,
                wk_ref, wvt_ref, qcos_ref, qsin_ref, kcos_ref, ksin_ref,
                xo_ref, qt_ref, k_ref, vt_ref):
    x = x_ref[...]
    mod = mod_ref[...]
    nw = nw_ref[...]
    x1 = _ffn_half_step(x, nw[0:1], mod[0:1], mod[1:2], mod[2:3], wi_ref, wo_ref)
    xo_ref[...] = x1

    h = _rmsnorm(x1, nw[1:2]) * (1.0 + mod[4:5]) + mod[3:4]
    pj = _dot(h.astype(BF16), wqkv_ref[...])
    nq = _rmsnorm(pj[:, :Q_RANK], qnw_ref[...]).astype(BF16)
    nkv = _rmsnorm(pj[:, Q_RANK:Q_RANK + KV_RANK], kvnw_ref[...]).astype(BF16)
    kp = pj[:, Q_RANK + KV_RANK:]

    qt = _dot_nt(wuqt_ref[...], nq)
    qcos = qcos_ref[...]
    qsin = qsin_ref[...]
    for hd in range(HEADS):
        r = hd * HEAD_PAD
        qt_ref[r:r + NOPE, :] = (qt[r:r + NOPE] * Q_SCALE).astype(BF16)
        rot = qt[r + NOPE:r + QK_DIM] * qcos + qt[r + QK_DIM:r + HEAD_PAD] * qsin
        qt_ref[r + NOPE:r + QK_DIM, :] = rot.astype(BF16)
        qt_ref[r + QK_DIM:r + HEAD_PAD, :] = jnp.zeros((HEAD_PAD - QK_DIM, qt.shape[1]), BF16)

    krot = kp * kcos_ref[...] + pltpu.roll(kp, HEAD_PAD - ROPE, axis=1) * ksin_ref[...]
    k_ref[...] = _dot(jnp.concatenate([nkv, krot.astype(BF16)], axis=1), wk_ref[...]).astype(BF16)

    vt = _dot_nt(wvt_ref[...], nkv)
    row = lax.broadcasted_iota(jnp.int32, vt.shape, 0)
    vt_ref[...] = jnp.where(row % V_ROWS == V_DIM, 1.0, vt).astype(BF16)


def _pre(x, mods, norm_w, w, tabs, n_lat_blocks):
    batch, tokens, _ = x.shape
    nb = tokens // TOKEN_BLOCK
    t = TOKEN_BLOCK

    def tok(b, i):
        return (b, i, 0)

    return pl.pallas_call(
        _pre_kernel,
        grid=(batch, nb),
        in_specs=[
            pl.BlockSpec((None, t, D_MODEL), tok),
            pl.BlockSpec((None, N_MOD, D_MODEL), lambda b, i: (jnp.where(i >= n_lat_blocks, batch, b), 0, 0)),
            _const_spec((3, D_MODEL)),
            _const_spec((D_MODEL, 2 * D_FF)),
            _const_spec((D_FF, D_MODEL)),
            _const_spec((D_MODEL, 512)),
            _const_spec((1, Q_RANK)),
            _const_spec((1, KV_RANK)),
            _const_spec((HEADS * HEAD_PAD, Q_RANK)),
            _const_spec((2 * KV_RANK, HEADS * HEAD_PAD)),
            _const_spec((HEADS * V_ROWS, KV_RANK)),
            pl.BlockSpec((ROPE, t), lambda b, i: (0, i)),
            pl.BlockSpec((ROPE, t), lambda b, i: (0, i)),
            pl.BlockSpec((t, HEAD_PAD), lambda b, i: (i, 0)),
            pl.BlockSpec((t, HEAD_PAD), lambda b, i: (i, 0)),
        ],
        out_specs=[
            pl.BlockSpec((None, t, D_MODEL), tok),
            pl.BlockSpec((None, HEADS * HEAD_PAD, t), lambda b, i: (b, 0, i)),
            pl.BlockSpec((None, t, HEADS * HEAD_PAD), tok),
            pl.BlockSpec((None, HEADS * V_ROWS, t), lambda b, i: (b, 0, i)),
        ],
        out_shape=[
            jax.ShapeDtypeStruct((batch, tokens, D_MODEL), F32),
            jax.ShapeDtypeStruct((batch, HEADS * HEAD_PAD, tokens), BF16),
            jax.ShapeDtypeStruct((batch, tokens, HEADS * HEAD_PAD), BF16),
            jax.ShapeDtypeStruct((batch, HEADS * V_ROWS, tokens), BF16),
        ],
        compiler_params=_params(),
        name="pre",
    )(x, mods, norm_w, w["wi1"], w["wo1"], w["wqkv"], w["qnw"], w["kvnw"], w["wuqt"], w["wk"], w["wvt"],
      tabs["qcos"], tabs["qsin"], tabs["kcos"], tabs["ksin"])


def _column_max(s, mx):
    parts = [s[r:r + MAX_ROWS] for r in range(0, s.shape[0], MAX_ROWS)]
    if mx is not None:
        parts.append(mx)
    while len(parts) > 1:
        nxt = [jnp.maximum(parts[j], parts[j + 1]) for j in range(0, len(parts) - 1, 2)]
        if len(parts) % 2:
            nxt.append(parts[-1])
        parts = nxt
    return parts[0]


def _scores(k_ref, qt, s_ref, m_ref, n_chunks, key_chunk):
    mx = None
    for c in range(n_chunks):
        rows = slice(c * key_chunk, (c + 1) * key_chunk)
        s = _dot(k_ref[rows, :], qt)
        s_ref[rows, :] = s
        mx = _column_max(s, mx)
    m_ref[...] = jnp.max(mx, axis=0, keepdims=True)


def _attn_kernel(qt0_ref, k0_ref, qtn_ref, kn_ref, vt_ref, o_ref, s_ref, m_ref, *, n_keys, key_chunk):
    n_chunks = n_keys // key_chunk
    tq = qtn_ref.shape[1]
    first = jnp.logical_and(jnp.logical_and(pl.program_id(0) == 0, pl.program_id(1) == 0),
                            pl.program_id(2) == 0)

    @pl.when(first)
    def _():
        _scores(k0_ref, qt0_ref[...], s_ref, m_ref, n_chunks, key_chunk)

    m = m_ref[...]
    qtn = qtn_ref[...]
    acc = jnp.zeros((V_ROWS, tq), F32)
    mx = None
    for c in range(n_chunks):
        rows = slice(c * key_chunk, (c + 1) * key_chunk)
        p = jnp.exp2(s_ref[rows, :] - m).astype(BF16)
        acc = acc + _dot(vt_ref[:, rows], p)
        s = _dot(kn_ref[rows, :], qtn)
        s_ref[rows, :] = s
        mx = _column_max(s, mx)
    m_ref[...] = jnp.max(mx, axis=0, keepdims=True)
    o_ref[...] = (acc[:V_DIM] * (1.0 / acc[V_DIM:V_DIM + 1])).astype(BF16)


def _attn(qt, k, vt, *, n_q_blocks):
    batch, _, tokens = qt.shape
    tq = QUERY_BLOCK
    assert tokens % KEY_CHUNK == 0

    def nxt(b, h, i):
        wrap_i = i == n_q_blocks - 1
        wrap_h = jnp.logical_and(wrap_i, h == HEADS - 1)
        last = jnp.logical_and(wrap_h, b == batch - 1)
        i2 = jnp.where(wrap_i, 0, i + 1)
        h2 = jnp.where(wrap_i, jnp.where(h == HEADS - 1, 0, h + 1), h)
        b2 = jnp.where(wrap_h, b + 1, b)
        return jnp.where(last, b, b2), jnp.where(last, h, h2), jnp.where(last, i, i2)

    def qt_next(b, h, i):
        b2, h2, i2 = nxt(b, h, i)
        return (b2, h2, i2)

    def k_next(b, h, i):
        b2, h2, _ = nxt(b, h, i)
        return (b2, 0, h2)

    kernel = functools.partial(_attn_kernel, n_keys=tokens, key_chunk=KEY_CHUNK)
    return pl.pallas_call(
        kernel,
        grid=(batch, HEADS, n_q_blocks),
        in_specs=[
            pl.BlockSpec((None, HEAD_PAD, tq), lambda b, h, i: (0, 0, 0), pipeline_mode=pl.Buffered(1)),
            pl.BlockSpec((None, tokens, HEAD_PAD), lambda b, h, i: (0, 0, 0), pipeline_mode=pl.Buffered(1)),
            pl.BlockSpec((None, HEAD_PAD, tq), qt_next),
            pl.BlockSpec((None, tokens, HEAD_PAD), k_next),
            pl.BlockSpec((None, V_ROWS, tokens), lambda b, h, i: (b, h, 0)),
        ],
        out_specs=pl.BlockSpec((None, V_DIM, tq), lambda b, h, i: (b, h, i)),
        out_shape=jax.ShapeDtypeStruct((batch, HEADS * V_DIM, tokens), BF16),
        scratch_shapes=[pltpu.VMEM((tokens, tq), F32), pltpu.VMEM((1, tq), F32)],
        compiler_params=pltpu.CompilerParams(
            dimension_semantics=("arbitrary", "arbitrary", "arbitrary"), vmem_limit_bytes=VMEM_LIMIT_BYTES),
        name="attn",
    )(qt, k, qt, k, vt)


def _attn_ctx_kernel(att_ref, qt_ref, k_ref, vt_ref, o_ref):
    del att_ref
    for hd in range(ATTN_HEADS_PER_STEP):
        s = _dot(k_ref[:, hd * HEAD_PAD:(hd + 1) * HEAD_PAD], qt_ref[hd * HEAD_PAD:(hd + 1) * HEAD_PAD, :])
        p = jnp.exp2(s - jnp.max(s, axis=0, keepdims=True)).astype(BF16)
        acc = _dot(vt_ref[hd * V_ROWS:(hd + 1) * V_ROWS, :], p)
        o_ref[hd * V_DIM:(hd + 1) * V_DIM, :] = (acc[:V_DIM] * (1.0 / acc[V_DIM:V_DIM + 1])).astype(BF16)


def _attn_ctx(att, qt, k, vt, *, n_ctx):
    batch, _, tokens = qt.shape
    hb = ATTN_HEADS_PER_STEP
    assert tokens % n_ctx == 0
    c = tokens // n_ctx - 1
    return pl.pallas_call(
        _attn_ctx_kernel,
        grid=(batch, HEADS // hb),
        in_specs=[
            pl.BlockSpec(memory_space=pl.ANY),
            pl.BlockSpec((None, hb * HEAD_PAD, n_ctx), lambda b, g: (b, g, c)),
            pl.BlockSpec((None, n_ctx, hb * HEAD_PAD), lambda b, g: (b, c, g)),
            pl.BlockSpec((None, hb * V_ROWS, n_ctx), lambda b, g: (b, g, c)),
        ],
        out_specs=pl.BlockSpec((None, hb * V_DIM, n_ctx), lambda b, g: (b, g, c)),
        out_shape=jax.ShapeDtypeStruct(att.shape, BF16),
        input_output_aliases={0: 0},
        compiler_params=_params(),
        name="attn_ctx",
    )(att, qt, k, vt)


def _post_kernel(x_ref, xp_ref, xn_ref, att_ref, mod_ref, nw_ref, wcv_ref, cw_ref, wout_ref, wi_ref, wo_ref,
                 fnw_ref, o_ref, xe_ref, u_ref, *, n_lat_blocks, n_blocks, final_norm):
    i = pl.program_id(1)
    t = TOKEN_BLOCK
    x = x_ref[...]
    mod = mod_ref[...]
    nw = nw_ref[...]

    xe_ref[0:HALO, :] = xp_ref[...]
    xe_ref[HALO:HALO + t, :] = x
    xe_ref[HALO + t:, :] = xn_ref[...]
    h = _rmsnorm(xe_ref[...], nw[1:2]) * (1.0 + mod[4:5]) + mod[3:4]
    pj = _dot(h.astype(BF16), wcv_ref[...])
    u_ref[...] = pj[:, CONV_DIM:2 * CONV_DIM] * pj[:, 2 * CONV_DIM:]
    left_ok = jnp.logical_and(i != 0, i != n_lat_blocks)
    right_ok = jnp.logical_and(i != n_lat_blocks - 1, i != n_blocks - 1)
    u_ref[0:HALO, :] = jnp.where(left_ok, u_ref[0:HALO, :], 0.0)
    u_ref[HALO + t:, :] = jnp.where(right_ok, u_ref[HALO + t:, :], 0.0)
    cw = cw_ref[...]
    conv = (u_ref[HALO - 1:HALO - 1 + t, :] * cw[0:1] + u_ref[HALO:HALO + t, :] * cw[1:2]
            + u_ref[HALO + 1:HALO + 1 + t, :] * cw[2:3])
    conv = pj[HALO:HALO + t, :CONV_DIM] * conv

    mix = _dot(jnp.concatenate([att_ref[...].T, conv.astype(BF16)], axis=1), wout_ref[...])
    x1 = x + mod[5:6] * mix
    x2 = _ffn_half_step(x1, nw[2:3], mod[6:7], mod[7:8], mod[8:9], wi_ref, wo_ref)
    if final_norm:
        x2 = _rmsnorm(x2, fnw_ref[...])
    o_ref[...] = x2


def _post(x, att, mods, norm_w, w, final_norm_w, *, n_lat_blocks, n_out_blocks, final_norm):
    batch, tokens, _ = x.shape
    t = TOKEN_BLOCK
    nb = tokens // t
    halo_per_block = t // HALO
    n_halo = tokens // HALO
    kernel = functools.partial(_post_kernel, n_lat_blocks=n_lat_blocks, n_blocks=nb, final_norm=final_norm)
    return pl.pallas_call(
        kernel,
        grid=(batch, n_out_blocks),
        in_specs=[
            pl.BlockSpec((None, t, D_MODEL), lambda b, i: (b, i, 0)),
            pl.BlockSpec((None, HALO, D_MODEL), lambda b, i: (b, jnp.maximum(i * halo_per_block - 1, 0), 0)),
            pl.BlockSpec((None, HALO, D_MODEL),
                         lambda b, i: (b, jnp.minimum((i + 1) * halo_per_block, n_halo - 1), 0)),
            pl.BlockSpec((None, HEADS * V_DIM, t), lambda b, i: (b, 0, i)),
            pl.BlockSpec((None, N_MOD, D_MODEL), lambda b, i: (jnp.where(i >= n_lat_blocks, batch, b), 0, 0)),
            _const_spec((3, D_MODEL)),
            _const_spec((D_MODEL, 3 * CONV_DIM)),
            _const_spec((3, CONV_DIM)),
            _const_spec((HEADS * V_DIM + CONV_DIM, D_MODEL)),
            _const_spec((D_MODEL, 2 * D_FF)),
            _const_spec((D_FF, D_MODEL)),
            _const_spec((1, D_MODEL)),
        ],
        out_specs=pl.BlockSpec((None, t, D_MODEL), lambda b, i: (b, i, 0)),
        out_shape=jax.ShapeDtypeStruct((batch, n_out_blocks * t, D_MODEL), F32),
        scratch_shapes=[pltpu.VMEM((t + 2 * HALO, D_MODEL), F32), pltpu.VMEM((t + 2 * HALO, CONV_DIM), F32)],
        compiler_params=_params(),
        name="post",
    )(x, x, x, att, mods, norm_w, w["wcv"], w["cw"], w["wout"], w["wi2"], w["wo2"], final_norm_w)


def _layer_weights(l, w_ffn1_in, w_ffn1_out, w_ffn2_in, w_ffn2_out, w_in, q_norm_w, kv_norm_w, w_uq, w_ukv,
                   conv_w, w_out):
    partner = jnp.array(_ROPE_PARTNER)
    w_in_l = w_in[l]
    kv0 = Q_RANK + KV_RANK
    kpe = w_in_l[:, kv0:kv0 + ROPE]
    wqkv = jnp.concatenate(
        [w_in_l[:, :kv0], kpe, kpe[:, partner], jnp.zeros((D_MODEL, HEAD_PAD - 2 * ROPE), F32)], axis=1)

    uq = w_uq[l].reshape(Q_RANK, HEADS, QK_DIM)
    uq = jnp.concatenate([uq, uq[:, :, NOPE:][:, :, partner]], axis=2)
    wuqt = uq.reshape(Q_RANK, HEADS * HEAD_PAD).T

    ukv = w_ukv[l].reshape(KV_RANK, HEADS, NOPE + V_DIM)
    wk_nope = jnp.concatenate([ukv[:, :, :NOPE], jnp.zeros((KV_RANK, HEADS, HEAD_PAD - NOPE), F32)], axis=2)
    place = jnp.zeros((KV_RANK, HEADS, HEAD_PAD), F32)
    place = place.at[jnp.arange(ROPE), :, NOPE + jnp.arange(ROPE)].set(1.0)
    wk = jnp.concatenate([wk_nope, place], axis=0).reshape(2 * KV_RANK, HEADS * HEAD_PAD)

    wv = jnp.concatenate([ukv[:, :, NOPE:], jnp.zeros((KV_RANK, HEADS, V_ROWS - V_DIM), F32)], axis=2)
    wvt = wv.reshape(KV_RANK, HEADS * V_ROWS).T

    return {
        "wi1": w_ffn1_in[l].astype(BF16), "wo1": w_ffn1_out[l].astype(BF16),
        "wi2": w_ffn2_in[l].astype(BF16), "wo2": w_ffn2_out[l].astype(BF16),
        "wqkv": wqkv.astype(BF16), "wcv": w_in_l[:, kv0 + ROPE:].astype(BF16),
        "qnw": q_norm_w[l].reshape(1, Q_RANK), "kvnw": kv_norm_w[l].reshape(1, KV_RANK),
        "wuqt": wuqt.astype(BF16), "wk": wk.astype(BF16), "wvt": wvt.astype(BF16),
        "cw": conv_w[l], "wout": w_out[l].astype(BF16),
    }


def _rope_tables(n_lat, n_ctx):
    t = jnp.arange(n_lat)
    row = (t // GRID_W).astype(F32)
    col = (t % GRID_W).astype(F32)
    d_axis = ROPE // 2
    inv = ROPE_BASE ** (-jnp.arange(0, d_axis, 2, dtype=F32) / d_axis)
    ar = row[:, None] * inv
    ac = col[:, None] * inv
    cr, sr, cc, sc = jnp.cos(ar), jnp.sin(ar), jnp.cos(ac), jnp.sin(ac)
    cos = jnp.concatenate([cr, cr, cc, cc], axis=1)
    sin = jnp.concatenate([-sr, sr, -sc, sc], axis=1)
    cos = jnp.concatenate([cos, jnp.ones((n_ctx, ROPE), F32)], axis=0)
    sin = jnp.concatenate([sin, jnp.zeros((n_ctx, ROPE), F32)], axis=0)
    pad = jnp.zeros((n_lat + n_ctx, HEAD_PAD - ROPE), F32)
    return {
        "qcos": (cos * Q_SCALE).T, "qsin": (sin * Q_SCALE).T,
        "kcos": jnp.concatenate([cos, pad], axis=1), "ksin": jnp.concatenate([sin, pad], axis=1),
    }


def kernel(x, c, ctx, c_ctx, w_ada, b_ada, norm_w, w_ffn1_in, w_ffn1_out, w_ffn2_in, w_ffn2_out, w_in, q_norm_w,
           kv_norm_w, w_uq, w_ukv, conv_w, w_out, final_norm_w):
    batch, n_lat, _ = x.shape
    n_ctx = ctx.shape[1]
    depth = w_ada.shape[0]
    assert n_ctx % TOKEN_BLOCK == 0 and n_lat % QUERY_BLOCK == 0 and QUERY_BLOCK % TOKEN_BLOCK == 0
    assert batch + 1 <= 8
    tokens = n_lat + n_ctx
    n_lat_blocks = n_lat // TOKEN_BLOCK
    n_blocks = tokens // TOKEN_BLOCK

    cond = jnp.concatenate([c, c_ctx[None, :], jnp.zeros((8 - batch - 1, D_MODEL), F32)], axis=0)
    mods = _ada(cond, w_ada, b_ada).reshape(depth, 8, N_MOD, D_MODEL)
    tabs = _rope_tables(n_lat, n_ctx)
    xs = jnp.concatenate([x, ctx], axis=1)
    fnw = final_norm_w.reshape(1, D_MODEL)

    for l in range(depth):
        last = l == depth - 1
        w = _layer_weights(l, w_ffn1_in, w_ffn1_out, w_ffn2_in, w_ffn2_out, w_in, q_norm_w, kv_norm_w, w_uq,
                           w_ukv, conv_w, w_out)
        xs, qt, k, vt = _pre(xs, mods[l], norm_w[l], w, tabs, n_lat_blocks)
        att = _attn(qt, k, vt, n_q_blocks=n_lat // QUERY_BLOCK)
        if not last:
            att = _attn_ctx(att, qt, k, vt, n_ctx=n_ctx)
        xs = _post(xs, att, mods[l], norm_w[l], w, fnw, n_lat_blocks=n_lat_blocks,
                   n_out_blocks=n_lat_blocks if last else n_blocks, final_norm=last)
    return xs
```

```python
import functools
import math

import jax
import jax.numpy as jnp
from jax import lax
from jax.experimental import pallas as pl
from jax.experimental.pallas import tpu as pltpu

F32 = jnp.float32
BF16 = jnp.bfloat16

D_MODEL = 1024
HEADS = 8
NOPE = 64
ROPE = 32
QK_DIM = NOPE + ROPE
V_DIM = 64
Q_RANK = 256
KV_RANK = 128
CONV_DIM = 512
D_FF = 2816
N_MOD = 9
GRID_W = 64
ROPE_BASE = 10000.0
EPS = 1e-6

HEAD_PAD = 128
V_ROWS = 80
TOKEN_BLOCK = 256
QUERY_BLOCK = 512
KEY_CHUNK = 256
MAX_ROWS = 32
HALO = 8
ATTN_HEADS_PER_STEP = 2
VMEM_LIMIT_BYTES = 56 * 1024 * 1024

Q_SCALE = math.log2(math.e) / math.sqrt(QK_DIM)

_ROPE_PARTNER = tuple(range(8, 16)) + tuple(range(0, 8)) + tuple(range(24, 32)) + tuple(range(16, 24))


def _dot(a, b):
    return jnp.dot(a, b, preferred_element_type=F32)


def _dot_nt(a, b):
    return lax.dot_general(a, b, (((1,), (1,)), ((), ())), preferred_element_type=F32)


def _rmsnorm(x, w):
    r = lax.rsqrt(jnp.mean(x * x, axis=-1, keepdims=True) + EPS)
    return (x * r) * w


def _const_spec(shape):
    return pl.BlockSpec(shape, lambda *_: (0,) * len(shape), pipeline_mode=pl.Buffered(1))


def _params():
    return pltpu.CompilerParams(
        dimension_semantics=("arbitrary", "arbitrary"), vmem_limit_bytes=VMEM_LIMIT_BYTES)


def _ada_kernel(c_ref, w_ref, b_ref, o_ref):
    c = c_ref[...]
    act = (c * (1.0 / (1.0 + jnp.exp(-c)))).astype(BF16)
    o_ref[0] = _dot(act, w_ref[0].astype(BF16)) + b_ref[0]


def _ada(cond, w_ada, b_ada):
    depth = w_ada.shape[0]
    n_out = w_ada.shape[2]
    blk = 1024
    return pl.pallas_call(
        _ada_kernel,
        grid=(depth, n_out // blk),
        in_specs=[
            pl.BlockSpec((8, D_MODEL), lambda l, j: (0, 0)),
            pl.BlockSpec((1, D_MODEL, blk), lambda l, j: (l, 0, j)),
            pl.BlockSpec((1, 1, blk), lambda l, j: (l, 0, j)),
        ],
        out_specs=pl.BlockSpec((1, 8, blk), lambda l, j: (l, 0, j)),
        out_shape=jax.ShapeDtypeStruct((depth, 8, n_out), F32),
        compiler_params=_params(),
        name="ada",
    )(cond, w_ada, b_ada.reshape(depth, 1, n_out))


def _ffn_half_step(x, norm_w, shift, scale, gate, wi_ref, wo_ref):
    h = _rmsnorm(x, norm_w) * (1.0 + scale) + shift
    gu = _dot(h.astype(BF16), wi_ref[...])
    g = gu[:, :D_FF]
    u = gu[:, D_FF:]
    a = (g * (1.0 / (1.0 + jnp.exp(-g))) * u).astype(BF16)
    return x + (0.5 * gate) * _dot(a, wo_ref[...])


def _pre_kernel(x_ref, mod_ref, nw_ref, wi_ref, wo_ref, wqkv_ref, qnw_ref, kvnw_ref, wuqt_ref,
                wk_ref, wvt_ref, qcos_ref, qsin_ref, kcos_ref, ksin_ref,
                xo_ref, qt_ref, k_ref, vt_ref):
    x = x_ref[...]
    mod = mod_ref[...]
    nw = nw_ref[...]
    x1 = _ffn_half_step(x, nw[0:1], mod[0:1], mod[1:2], mod[2:3], wi_ref, wo_ref)
    xo_ref[...] = x1

    h = _rmsnorm(x1, nw[1:2]) * (1.0 + mod[4:5]) + mod[3:4]
    pj = _dot(h.astype(BF16), wqkv_ref[...])
    nq = _rmsnorm(pj[:, :Q_RANK], qnw_ref[...]).astype(BF16)
    nkv = _rmsnorm(pj[:, Q_RANK:Q_RANK + KV_RANK], kvnw_ref[...]).astype(BF16)
    kp = pj[:, Q_RANK + KV_RANK:]

    qt = _dot_nt(wuqt_ref[...], nq)
    qcos = qcos_ref[...]
    qsin = qsin_ref[...]
    for hd in range(HEADS):
        r = hd * HEAD_PAD
        qt_ref[r:r + NOPE, :] = (qt[r:r + NOPE] * Q_SCALE).astype(BF16)
        rot = qt[r + NOPE:r + QK_DIM] * qcos + qt[r + QK_DIM:r + HEAD_PAD] * qsin
        qt_ref[r + NOPE:r + QK_DIM, :] = rot.astype(BF16)
        qt_ref[r + QK_DIM:r + HEAD_PAD, :] = jnp.zeros((HEAD_PAD - QK_DIM, qt.shape[1]), BF16)

    krot = kp * kcos_ref[...] + pltpu.roll(kp, HEAD_PAD - ROPE, axis=1) * ksin_ref[...]
    k_ref[...] = _dot(jnp.concatenate([nkv, krot.astype(BF16)], axis=1), wk_ref[...]).astype(BF16)

    vt = _dot_nt(wvt_ref[...], nkv)
    row = lax.broadcasted_iota(jnp.int32, vt.shape, 0)
    vt_ref[...] = jnp.where(row % V_ROWS == V_DIM, 1.0, vt).astype(BF16)


def _pre(x, mods, norm_w, w, tabs, n_lat_blocks):
    batch, tokens, _ = x.shape
    nb = tokens // TOKEN_BLOCK
    t = TOKEN_BLOCK

    def tok(b, i):
        return (b, i, 0)

    return pl.pallas_call(
        _pre_kernel,
        grid=(batch, nb),
        in_specs=[
            pl.BlockSpec((None, t, D_MODEL), tok),
            pl.BlockSpec((None, N_MOD, D_MODEL), lambda b, i: (jnp.where(i >= n_lat_blocks, batch, b), 0, 0)),
            _const_spec((3, D_MODEL)),
            _const_spec((D_MODEL, 2 * D_FF)),
            _const_spec((D_FF, D_MODEL)),
            _const_spec((D_MODEL, 512)),
            _const_spec((1, Q_RANK)),
            _const_spec((1, KV_RANK)),
            _const_spec((HEADS * HEAD_PAD, Q_RANK)),
            _const_spec((2 * KV_RANK, HEADS * HEAD_PAD)),
            _const_spec((HEADS * V_ROWS, KV_RANK)),
            pl.BlockSpec((ROPE, t), lambda b, i: (0, i)),
            pl.BlockSpec((ROPE, t), lambda b, i: (0, i)),
            pl.BlockSpec((t, HEAD_PAD), lambda b, i: (i, 0)),
            pl.BlockSpec((t, HEAD_PAD), lambda b, i: (i, 0)),
        ],
        out_specs=[
            pl.BlockSpec((None, t, D_MODEL), tok),
            pl.BlockSpec((None, HEADS * HEAD_PAD, t), lambda b, i: (b, 0, i)),
            pl.BlockSpec((None, t, HEADS * HEAD_PAD), tok),
            pl.BlockSpec((None, HEADS * V_ROWS, t), lambda b, i: (b, 0, i)),
        ],
        out_shape=[
            jax.ShapeDtypeStruct((batch, tokens, D_MODEL), F32),
            jax.ShapeDtypeStruct((batch, HEADS * HEAD_PAD, tokens), BF16),
            jax.ShapeDtypeStruct((batch, tokens, HEADS * HEAD_PAD), BF16),
            jax.ShapeDtypeStruct((batch, HEADS * V_ROWS, tokens), BF16),
        ],
        compiler_params=_params(),
        name="pre",
    )(x, mods, norm_w, w["wi1"], w["wo1"], w["wqkv"], w["qnw"], w["kvnw"], w["wuqt"], w["wk"], w["wvt"],
      tabs["qcos"], tabs["qsin"], tabs["kcos"], tabs["ksin"])


def _column_max(s, mx):
    parts = [s[r:r + MAX_ROWS] for r in range(0, s.shape[0], MAX_ROWS)]
    if mx is not None:
        parts.append(mx)
    while len(parts) > 1:
        nxt = [jnp.maximum(parts[j], parts[j + 1]) for j in range(0, len(parts) - 1, 2)]
        if len(parts) % 2:
            nxt.append(parts[-1])
        parts = nxt
    return parts[0]


def _scores(k_ref, qt, s_ref, m_ref, n_chunks, key_chunk):
    mx = None
    for c in range(n_chunks):
        rows = slice(c * key_chunk, (c + 1) * key_chunk)
        s = _dot(k_ref[rows, :], qt)
        s_ref[rows, :] = s
        mx = _column_max(s, mx)
    m_ref[...] = jnp.max(mx, axis=0, keepdims=True)


def _attn_kernel(qt0_ref, k0_ref, qtn_ref, kn_ref, vt_ref, o_ref, s_ref, m_ref, *, n_keys, key_chunk):
    n_chunks = n_keys // key_chunk
    tq = qtn_ref.shape[1]
    first = jnp.logical_and(jnp.logical_and(pl.program_id(0) == 0, pl.program_id(1) == 0),
                            pl.program_id(2) == 0)

    @pl.when(first)
    def _():
        _scores(k0_ref, qt0_ref[...], s_ref, m_ref, n_chunks, key_chunk)

    m = m_ref[...]
    qtn = qtn_ref[...]
    acc = jnp.zeros((V_ROWS, tq), F32)
    mx = None
    for c in range(n_chunks):
        rows = slice(c * key_chunk, (c + 1) * key_chunk)
        p = jnp.exp2(s_ref[rows, :] - m).astype(BF16)
        acc = acc + _dot(vt_ref[:, rows], p)
        s = _dot(kn_ref[rows, :], qtn)
        s_ref[rows, :] = s
        mx = _column_max(s, mx)
    m_ref[...] = jnp.max(mx, axis=0, keepdims=True)
    o_ref[...] = (acc[:V_DIM] * (1.0 / acc[V_DIM:V_DIM + 1])).astype(BF16)


def _attn(qt, k, vt, *, n_q_blocks):
    batch, _, tokens = qt.shape
    tq = QUERY_BLOCK
    assert tokens % KEY_CHUNK == 0

    def nxt(b, h, i):
        wrap_i = i == n_q_blocks - 1
        wrap_h = jnp.logical_and(wrap_i, h == HEADS - 1)
        last = jnp.logical_and(wrap_h, b == batch - 1)
        i2 = jnp.where(wrap_i, 0, i + 1)
        h2 = jnp.where(wrap_i, jnp.where(h == HEADS - 1, 0, h + 1), h)
        b2 = jnp.where(wrap_h, b + 1, b)
        return jnp.where(last, b, b2), jnp.where(last, h, h2), jnp.where(last, i, i2)

    def qt_next(b, h, i):
        b2, h2, i2 = nxt(b, h, i)
        return (b2, h2, i2)

    def k_next(b, h, i):
        b2, h2, _ = nxt(b, h, i)
        return (b2, 0, h2)

    kernel = functools.partial(_attn_kernel, n_keys=tokens, key_chunk=KEY_CHUNK)
    return pl.pallas_call(
        kernel,
        grid=(batch, HEADS, n_q_blocks),
        in_specs=[
            pl.BlockSpec((None, HEAD_PAD, tq), lambda b, h, i: (0, 0, 0), pipeline_mode=pl.Buffered(1)),
            pl.BlockSpec((None, tokens, HEAD_PAD), lambda b, h, i: (0, 0, 0), pipeline_mode=pl.Buffered(1)),
            pl.BlockSpec((None, HEAD_PAD, tq), qt_next),
            pl.BlockSpec((None, tokens, HEAD_PAD), k_next),
            pl.BlockSpec((None, V_ROWS, tokens), lambda b, h, i: (b, h, 0)),
        ],
        out_specs=pl.BlockSpec((None, V_DIM, tq), lambda b, h, i: (b, h, i)),
        out_shape=jax.ShapeDtypeStruct((batch, HEADS * V_DIM, tokens), BF16),
        scratch_shapes=[pltpu.VMEM((tokens, tq), F32), pltpu.VMEM((1, tq), F32)],
        compiler_params=pltpu.CompilerParams(
            dimension_semantics=("arbitrary", "arbitrary", "arbitrary"), vmem_limit_bytes=VMEM_LIMIT_BYTES),
        name="attn",
    )(qt, k, qt, k, vt)


def _attn_ctx_kernel(att_ref, qt_ref, k_ref, vt_ref, o_ref):
    del att_ref
    for hd in range(ATTN_HEADS_PER_STEP):
        s = _dot(k_ref[:, hd * HEAD_PAD:(hd + 1) * HEAD_PAD], qt_ref[hd * HEAD_PAD:(hd + 1) * HEAD_PAD, :])
        p = jnp.exp2(s - jnp.max(s, axis=0, keepdims=True)).astype(BF16)
        acc = _dot(vt_ref[hd * V_ROWS:(hd + 1) * V_ROWS, :], p)
        o_ref[hd * V_DIM:(hd + 1) * V_DIM, :] = (acc[:V_DIM] * (1.0 / acc[V_DIM:V_DIM + 1])).astype(BF16)


def _attn_ctx(att, qt, k, vt, *, n_ctx):
    batch, _, tokens = qt.shape
    hb = ATTN_HEADS_PER_STEP
    assert tokens % n_ctx == 0
    c = tokens // n_ctx - 1
    return pl.pallas_call(
        _attn_ctx_kernel,
        grid=(batch, HEADS // hb),
        in_specs=[
            pl.BlockSpec(memory_space=pl.ANY),
            pl.BlockSpec((None, hb * HEAD_PAD, n_ctx), lambda b, g: (b, g, c)),
            pl.BlockSpec((None, n_ctx, hb * HEAD_PAD), lambda b, g: (b, c, g)),
            pl.BlockSpec((None, hb * V_ROWS, n_ctx), lambda b, g: (b, g, c)),
        ],
        out_specs=pl.BlockSpec((None, hb * V_DIM, n_ctx), lambda b, g: (b, g, c)),
        out_shape=jax.ShapeDtypeStruct(att.shape, BF16),
        input_output_aliases={0: 0},
        compiler_params=_params(),
        name="attn_ctx",
    )(att, qt, k, vt)


def _post_kernel(x_ref, xp_ref, xn_ref, att_ref, mod_ref, nw_ref, wcv_ref, cw_ref, wout_ref, wi_ref, wo_ref,
                 fnw_ref, o_ref, xe_ref, u_ref, *, n_lat_blocks, n_blocks, final_norm):
    i = pl.program_id(1)
    t = TOKEN_BLOCK
    x = x_ref[...]
    mod = mod_ref[...]
    nw = nw_ref[...]

    xe_ref[0:HALO, :] = xp_ref[...]
    xe_ref[HALO:HALO + t, :] = x
    xe_ref[HALO + t:, :] = xn_ref[...]
    h = _rmsnorm(xe_ref[...], nw[1:2]) * (1.0 + mod[4:5]) + mod[3:4]
    pj = _dot(h.astype(BF16), wcv_ref[...])
    u_ref[...] = pj[:, CONV_DIM:2 * CONV_DIM] * pj[:, 2 * CONV_DIM:]
    left_ok = jnp.logical_and(i != 0, i != n_lat_blocks)
    right_ok = jnp.logical_and(i != n_lat_blocks - 1, i != n_blocks - 1)
    u_ref[0:HALO, :] = jnp.where(left_ok, u_ref[0:HALO, :], 0.0)
    u_ref[HALO + t:, :] = jnp.where(right_ok, u_ref[HALO + t:, :], 0.0)
    cw = cw_ref[...]
    conv = (u_ref[HALO - 1:HALO - 1 + t, :] * cw[0:1] + u_ref[HALO:HALO + t, :] * cw[1:2]
            + u_ref[HALO + 1:HALO + 1 + t, :] * cw[2:3])
    conv = pj[HALO:HALO + t, :CONV_DIM] * conv

    mix = _dot(jnp.concatenate([att_ref[...].T, conv.astype(BF16)], axis=1), wout_ref[...])
    x1 = x + mod[5:6] * mix
    x2 = _ffn_half_step(x1, nw[2:3], mod[6:7], mod[7:8], mod[8:9], wi_ref, wo_ref)
    if final_norm:
        x2 = _rmsnorm(x2, fnw_ref[...])
    o_ref[...] = x2


def _post(x, att, mods, norm_w, w, final_norm_w, *, n_lat_blocks, n_out_blocks, final_norm):
    batch, tokens, _ = x.shape
    t = TOKEN_BLOCK
    nb = tokens // t
    halo_per_block = t // HALO
    n_halo = tokens // HALO
    kernel = functools.partial(_post_kernel, n_lat_blocks=n_lat_blocks, n_blocks=nb, final_norm=final_norm)
    return pl.pallas_call(
        kernel,
        grid=(batch, n_out_blocks),
        in_specs=[
            pl.BlockSpec((None, t, D_MODEL), lambda b, i: (b, i, 0)),
            pl.BlockSpec((None, HALO, D_MODEL), lambda b, i: (b, jnp.maximum(i * halo_per_block - 1, 0), 0)),
            pl.BlockSpec((None, HALO, D_MODEL),
                         lambda b, i: (b, jnp.minimum((i + 1) * halo_per_block, n_halo - 1), 0)),
            pl.BlockSpec((None, HEADS * V_DIM, t), lambda b, i: (b, 0, i)),
            pl.BlockSpec((None, N_MOD, D_MODEL), lambda b, i: (jnp.where(i >= n_lat_blocks, batch, b), 0, 0)),
            _const_spec((3, D_MODEL)),
            _const_spec((D_MODEL, 3 * CONV_DIM)),
            _const_spec((3, CONV_DIM)),
            _const_spec((HEADS * V_DIM + CONV_DIM, D_MODEL)),
            _const_spec((D_MODEL, 2 * D_FF)),
            _const_spec((D_FF, D_MODEL)),
            _const_spec((1, D_MODEL)),
        ],
        out_specs=pl.BlockSpec((None, t, D_MODEL), lambda b, i: (b, i, 0)),
        out_shape=jax.ShapeDtypeStruct((batch, n_out_blocks * t, D_MODEL), F32),
        scratch_shapes=[pltpu.VMEM((t + 2 * HALO, D_MODEL), F32), pltpu.VMEM((t + 2 * HALO, CONV_DIM), F32)],
        compiler_params=_params(),
        name="post",
    )(x, x, x, att, mods, norm_w, w["wcv"], w["cw"], w["wout"], w["wi2"], w["wo2"], final_norm_w)


def _layer_weights(l, w_ffn1_in, w_ffn1_out, w_ffn2_in, w_ffn2_out, w_in, q_norm_w, kv_norm_w, w_uq, w_ukv,
                   conv_w, w_out):
    partner = jnp.array(_ROPE_PARTNER)
    w_in_l = w_in[l]
    kv0 = Q_RANK + KV_RANK
    kpe = w_in_l[:, kv0:kv0 + ROPE]
    wqkv = jnp.concatenate(
        [w_in_l[:, :kv0], kpe, kpe[:, partner], jnp.zeros((D_MODEL, HEAD_PAD - 2 * ROPE), F32)], axis=1)

    uq = w_uq[l].reshape(Q_RANK, HEADS, QK_DIM)
    uq = jnp.concatenate([uq, uq[:, :, NOPE:][:, :, partner]], axis=2)
    wuqt = uq.reshape(Q_RANK, HEADS * HEAD_PAD).T

    ukv = w_ukv[l].reshape(KV_RANK, HEADS, NOPE + V_DIM)
    wk_nope = jnp.concatenate([ukv[:, :, :NOPE], jnp.zeros((KV_RANK, HEADS, HEAD_PAD - NOPE), F32)], axis=2)
    place = jnp.zeros((KV_RANK, HEADS, HEAD_PAD), F32)
    place = place.at[jnp.arange(ROPE), :, NOPE + jnp.arange(ROPE)].set(1.0)
    wk = jnp.concatenate([wk_nope, place], axis=0).reshape(2 * KV_RANK, HEADS * HEAD_PAD)

    wv = jnp.concatenate([ukv[:, :, NOPE:], jnp.zeros((KV_RANK, HEADS, V_ROWS - V_DIM), F32)], axis=2)
    wvt = wv.reshape(KV_RANK, HEADS * V_ROWS).T

    return {
        "wi1": w_ffn1_in[l].astype(BF16), "wo1": w_ffn1_out[l].astype(BF16),
        "wi2": w_ffn2_in[l].astype(BF16), "wo2": w_ffn2_out[l].astype(BF16),
        "wqkv": wqkv.astype(BF16), "wcv": w_in_l[:, kv0 + ROPE:].astype(BF16),
        "qnw": q_norm_w[l].reshape(1, Q_RANK), "kvnw": kv_norm_w[l].reshape(1, KV_RANK),
        "wuqt": wuqt.astype(BF16), "wk": wk.astype(BF16), "wvt": wvt.astype(BF16),
        "cw": conv_w[l], "wout": w_out[l].astype(BF16),
    }


def _rope_tables(n_lat, n_ctx):
    t = jnp.arange(n_lat)
    row = (t // GRID_W).astype(F32)
    col = (t % GRID_W).astype(F32)
    d_axis = ROPE // 2
    inv = ROPE_BASE ** (-jnp.arange(0, d_axis, 2, dtype=F32) / d_axis)
    ar = row[:, None] * inv
    ac = col[:, None] * inv
    cr, sr, cc, sc = jnp.cos(ar), jnp.sin(ar), jnp.cos(ac), jnp.sin(ac)
    cos = jnp.concatenate([cr, cr, cc, cc], axis=1)
    sin = jnp.concatenate([-sr, sr, -sc, sc], axis=1)
    cos = jnp.concatenate([cos, jnp.ones((n_ctx, ROPE), F32)], axis=0)
    sin = jnp.concatenate([sin, jnp.zeros((n_ctx, ROPE), F32)], axis=0)
    pad = jnp.zeros((n_lat + n_ctx, HEAD_PAD - ROPE), F32)
    return {
        "qcos": (cos * Q_SCALE).T, "qsin": (sin * Q_SCALE).T,
        "kcos": jnp.concatenate([cos, pad], axis=1), "ksin": jnp.concatenate([sin, pad], axis=1),
    }


def kernel(x, c, ctx, c_ctx, w_ada, b_ada, norm_w, w_ffn1_in, w_ffn1_out, w_ffn2_in, w_ffn2_out, w_in, q_norm_w,
           kv_norm_w, w_uq, w_ukv, conv_w, w_out, final_norm_w):
    batch, n_lat, _ = x.shape
    n_ctx = ctx.shape[1]
    depth = w_ada.shape[0]
    assert n_ctx % TOKEN_BLOCK == 0 and n_lat % QUERY_BLOCK == 0 and QUERY_BLOCK % TOKEN_BLOCK == 0
    assert batch + 1 <= 8
    tokens = n_lat + n_ctx
    n_lat_blocks = n_lat // TOKEN_BLOCK
    n_blocks = tokens // TOKEN_BLOCK

    cond = jnp.concatenate([c, c_ctx[None, :], jnp.zeros((8 - batch - 1, D_MODEL), F32)], axis=0)
    mods = _ada(cond, w_ada, b_ada).reshape(depth, 8, N_MOD, D_MODEL)
    tabs = _rope_tables(n_lat, n_ctx)
    xs = jnp.concatenate([x, ctx], axis=1)
    fnw = final_norm_w.reshape(1, D_MODEL)

    for l in range(depth):
        last = l == depth - 1
        w = _layer_weights(l, w_ffn1_in, w_ffn1_out, w_ffn2_in, w_ffn2_out, w_in, q_norm_w, kv_norm_w, w_uq,
                           w_ukv, conv_w, w_out)
        xs, qt, k, vt = _pre(xs, mods[l], norm_w[l], w, tabs, n_lat_blocks)
        att = _attn(qt, k, vt, n_q_blocks=n_lat // QUERY_BLOCK)
        if not last:
            att = _attn_ctx(att, qt, k, vt, n_ctx=n_ctx)
        xs = _post(xs, att, mods[l], norm_w[l], w, fnw, n_lat_blocks=n_lat_blocks,
                   n_out_blocks=n_lat_blocks if last else n_blocks, final_norm=last)
    return xs
```

```python
import functools
import math

import jax
import jax.numpy as jnp
import numpy as np
from jax import lax
from jax.experimental import pallas as pl
from jax.experimental.pallas import tpu as pltpu

F32 = jnp.float32
BF16 = jnp.bfloat16

D_MODEL = 1024
HEADS = 8
NOPE = 64
ROPE = 32
QK_DIM = NOPE + ROPE
V_DIM = 64
Q_RANK = 256
KV_RANK = 128
CONV_DIM = 512
D_FF = 2816
N_MOD = 9
GRID_W = 64
ROPE_BASE = 10000.0
EPS = 1e-6

HEAD_PAD = 128
V_ROWS = 80
TOKEN_BLOCK = 256
QUERY_BLOCK = 512
KEY_CHUNK = 256
MAX_ROWS = 32
HALO = 8
ATTN_HEADS_PER_STEP = 2
VMEM_LIMIT_BYTES = 56 * 1024 * 1024

Q_SCALE = math.log2(math.e) / math.sqrt(QK_DIM)

_ROPE_PARTNER = tuple(range(8, 16)) + tuple(range(0, 8)) + tuple(range(24, 32)) + tuple(range(16, 24))


def _dot(a, b):
    return jnp.dot(a, b, preferred_element_type=F32)


def _dot_nt(a, b):
    return lax.dot_general(a, b, (((1,), (1,)), ((), ())), preferred_element_type=F32)


def _rmsnorm(x, w):
    r = lax.rsqrt(jnp.mean(x * x, axis=-1, keepdims=True) + EPS)
    return (x * r) * w


def _const_spec(shape):
    return pl.BlockSpec(shape, lambda *_: (0,) * len(shape), pipeline_mode=pl.Buffered(1))


def _layer_spec(shape, layer):
    return pl.BlockSpec((None,) + shape, lambda *_: (layer,) + (0,) * len(shape), pipeline_mode=pl.Buffered(1))


def _params():
    return pltpu.CompilerParams(
        dimension_semantics=("arbitrary", "arbitrary"), vmem_limit_bytes=VMEM_LIMIT_BYTES)


def _ada_kernel(c_ref, w_ref, b_ref, o_ref):
    c = c_ref[...]
    act = (c * (1.0 / (1.0 + jnp.exp(-c)))).astype(BF16)
    o_ref[0] = _dot(act, w_ref[0].astype(BF16)) + b_ref[0]


def _ada(cond, w_ada, b_ada):
    depth = w_ada.shape[0]
    n_out = w_ada.shape[2]
    blk = 1024
    return pl.pallas_call(
        _ada_kernel,
        grid=(depth, n_out // blk),
        in_specs=[
            pl.BlockSpec((8, D_MODEL), lambda l, j: (0, 0)),
            pl.BlockSpec((1, D_MODEL, blk), lambda l, j: (l, 0, j)),
            pl.BlockSpec((1, 1, blk), lambda l, j: (l, 0, j)),
        ],
        out_specs=pl.BlockSpec((1, 8, blk), lambda l, j: (l, 0, j)),
        out_shape=jax.ShapeDtypeStruct((depth, 8, n_out), F32),
        compiler_params=_params(),
        name="ada",
    )(cond, w_ada, b_ada.reshape(depth, 1, n_out))


def _ffn_half_step(x, norm_w, shift, scale, gate, wi_ref, wo_ref):
    h = _rmsnorm(x, norm_w) * (1.0 + scale) + shift
    gu = _dot(h.astype(BF16), wi_ref[...])
    g = gu[:, :D_FF]
    u = gu[:, D_FF:]
    a = (g * (1.0 / (1.0 + jnp.exp(-g))) * u).astype(BF16)
    return x + (0.5 * gate) * _dot(a, wo_ref[...])


def _pre_kernel(*refs, n_lat_blocks, split_input):
    if split_input:
        x_ref, xc_ref, *refs = refs
        x = jnp.where(pl.program_id(1) >= n_lat_blocks, xc_ref[...], x_ref[...])
    else:
        x_ref, *refs = refs
        x = x_ref[...]
    (mod_ref, nw_ref, wi_ref, wo_ref, wqkv_ref, qnw_ref, kvnw_ref, wuqt_ref, wk_ref, wvt_ref,
     qcos_ref, qsin_ref, kcos_ref, ksin_ref, xo_ref, qt_ref, k_ref, vt_ref) = refs
    mod = mod_ref[...]
    nw = nw_ref[...]
    x1 = _ffn_half_step(x, nw[0:1], mod[0:1], mod[1:2], mod[2:3], wi_ref, wo_ref)
    xo_ref[...] = x1

    h = _rmsnorm(x1, nw[1:2]) * (1.0 + mod[4:5]) + mod[3:4]
    pj = _dot(h.astype(BF16), wqkv_ref[...])
    nq = _rmsnorm(pj[:, :Q_RANK], qnw_ref[...]).astype(BF16)
    nkv = _rmsnorm(pj[:, Q_RANK:Q_RANK + KV_RANK], kvnw_ref[...]).astype(BF16)
    kp = pj[:, Q_RANK + KV_RANK:]

    qt = _dot_nt(wuqt_ref[...], nq)
    qcos = qcos_ref[...]
    qsin = qsin_ref[...]
    for hd in range(HEADS):
        r = hd * HEAD_PAD
        qt_ref[r:r + NOPE, :] = (qt[r:r + NOPE] * Q_SCALE).astype(BF16)
        rot = qt[r + NOPE:r + QK_DIM] * qcos + qt[r + QK_DIM:r + HEAD_PAD] * qsin
        qt_ref[r + NOPE:r + QK_DIM, :] = rot.astype(BF16)
        qt_ref[r + QK_DIM:r + HEAD_PAD, :] = jnp.zeros((HEAD_PAD - QK_DIM, qt.shape[1]), BF16)

    krot = kp * kcos_ref[...] + pltpu.roll(kp, HEAD_PAD - ROPE, axis=1) * ksin_ref[...]
    k_ref[...] = _dot(jnp.concatenate([nkv, krot.astype(BF16)], axis=1), wk_ref[...]).astype(BF16)

    vt = _dot_nt(wvt_ref[...], nkv)
    row = lax.broadcasted_iota(jnp.int32, vt.shape, 0)
    vt_ref[...] = jnp.where(row % V_ROWS == V_DIM, 1.0, vt).astype(BF16)


def _pre(x, x_ctx, mods, norm_w, w, tabs, layer, n_lat_blocks):
    batch = x.shape[0]
    t = TOKEN_BLOCK
    split_input = x_ctx is not None
    tokens = x.shape[1] + (x_ctx.shape[1] if split_input else 0)
    nb = tokens // t

    def tok(b, i):
        return (b, i, 0)

    if split_input:
        x_args = [x, x_ctx]
        x_specs = [pl.BlockSpec((None, t, D_MODEL), lambda b, i: (b, jnp.minimum(i, n_lat_blocks - 1), 0)),
                   pl.BlockSpec((None, t, D_MODEL), lambda b, i: (b, jnp.maximum(i - n_lat_blocks, 0), 0))]
    else:
        x_args = [x]
        x_specs = [pl.BlockSpec((None, t, D_MODEL), tok)]

    return pl.pallas_call(
        functools.partial(_pre_kernel, n_lat_blocks=n_lat_blocks, split_input=split_input),
        grid=(batch, nb),
        in_specs=x_specs + [
            pl.BlockSpec((None, N_MOD, D_MODEL), lambda b, i: (jnp.where(i >= n_lat_blocks, batch, b), 0, 0)),
            _const_spec((3, D_MODEL)),
            _layer_spec((D_MODEL, 2 * D_FF), layer),
            _layer_spec((D_FF, D_MODEL), layer),
            _const_spec((D_MODEL, 512)),
            _const_spec((1, Q_RANK)),
            _const_spec((1, KV_RANK)),
            _const_spec((HEADS * HEAD_PAD, Q_RANK)),
            _const_spec((2 * KV_RANK, HEADS * HEAD_PAD)),
            _const_spec((HEADS * V_ROWS, KV_RANK)),
            pl.BlockSpec((ROPE, t), lambda b, i: (0, i)),
            pl.BlockSpec((ROPE, t), lambda b, i: (0, i)),
            pl.BlockSpec((t, HEAD_PAD), lambda b, i: (i, 0)),
            pl.BlockSpec((t, HEAD_PAD), lambda b, i: (i, 0)),
        ],
        out_specs=[
            pl.BlockSpec((None, t, D_MODEL), tok),
            pl.BlockSpec((None, HEADS * HEAD_PAD, t), lambda b, i: (b, 0, i)),
            pl.BlockSpec((None, t, HEADS * HEAD_PAD), tok),
            pl.BlockSpec((None, HEADS * V_ROWS, t), lambda b, i: (b, 0, i)),
        ],
        out_shape=[
            jax.ShapeDtypeStruct((batch, tokens, D_MODEL), F32),
            jax.ShapeDtypeStruct((batch, HEADS * HEAD_PAD, tokens), BF16),
            jax.ShapeDtypeStruct((batch, tokens, HEADS * HEAD_PAD), BF16),
            jax.ShapeDtypeStruct((batch, HEADS * V_ROWS, tokens), BF16),
        ],
        compiler_params=_params(),
        name="pre",
    )(*x_args, mods, norm_w, w["wi1"], w["wo1"], w["wqkv"], w["qnw"], w["kvnw"], w["wuqt"], w["wk"], w["wvt"],
      tabs["qcos"], tabs["qsin"], tabs["kcos"], tabs["ksin"])


def _column_max(s, mx):
    parts = [s[r:r + MAX_ROWS] for r in range(0, s.shape[0], MAX_ROWS)]
    if mx is not None:
        parts.append(mx)
    while len(parts) > 1:
        nxt = [jnp.maximum(parts[j], parts[j + 1]) for j in range(0, len(parts) - 1, 2)]
        if len(parts) % 2:
            nxt.append(parts[-1])
        parts = nxt
    return parts[0]


def _scores(k_ref, qt, s_ref, m_ref, n_chunks, key_chunk):
    mx = None
    for c in range(n_chunks):
        rows = slice(c * key_chunk, (c + 1) * key_chunk)
        s = _dot(k_ref[rows, :], qt)
        s_ref[rows, :] = s
        mx = _column_max(s, mx)
    m_ref[...] = jnp.max(mx, axis=0, keepdims=True)


def _attn_kernel(qt0_ref, k0_ref, qtn_ref, kn_ref, vt_ref, o_ref, s_ref, m_ref, *, n_keys, key_chunk):
    n_chunks = n_keys // key_chunk
    tq = qtn_ref.shape[1]
    first = jnp.logical_and(jnp.logical_and(pl.program_id(0) == 0, pl.program_id(1) == 0),
                            pl.program_id(2) == 0)

    @pl.when(first)
    def _():
        _scores(k0_ref, qt0_ref[...], s_ref, m_ref, n_chunks, key_chunk)

    m = m_ref[...]
    qtn = qtn_ref[...]
    acc = jnp.zeros((V_ROWS, tq), F32)
    mx = None
    for c in range(n_chunks):
        rows = slice(c * key_chunk, (c + 1) * key_chunk)
        p = jnp.exp2(s_ref[rows, :] - m).astype(BF16)
        acc = acc + _dot(vt_ref[:, rows], p)
        s = _dot(kn_ref[rows, :], qtn)
        s_ref[rows, :] = s
        mx = _column_max(s, mx)
    m_ref[...] = jnp.max(mx, axis=0, keepdims=True)
    o_ref[...] = (acc[:V_DIM] * (1.0 / acc[V_DIM:V_DIM + 1])).astype(BF16)


def _attn(qt, k, vt, *, n_q_blocks):
    batch, _, tokens = qt.shape
    tq = QUERY_BLOCK
    assert tokens % KEY_CHUNK == 0

    def nxt(b, h, i):
        wrap_i = i == n_q_blocks - 1
        wrap_h = jnp.logical_and(wrap_i, h == HEADS - 1)
        last = jnp.logical_and(wrap_h, b == batch - 1)
        i2 = jnp.where(wrap_i, 0, i + 1)
        h2 = jnp.where(wrap_i, jnp.where(h == HEADS - 1, 0, h + 1), h)
        b2 = jnp.where(wrap_h, b + 1, b)
        return jnp.where(last, b, b2), jnp.where(last, h, h2), jnp.where(last, i, i2)

    def qt_next(b, h, i):
        b2, h2, i2 = nxt(b, h, i)
        return (b2, h2, i2)

    def k_next(b, h, i):
        b2, h2, _ = nxt(b, h, i)
        return (b2, 0, h2)

    kernel = functools.partial(_attn_kernel, n_keys=tokens, key_chunk=KEY_CHUNK)
    return pl.pallas_call(
        kernel,
        grid=(batch, HEADS, n_q_blocks),
        in_specs=[
            pl.BlockSpec((None, HEAD_PAD, tq), lambda b, h, i: (0, 0, 0), pipeline_mode=pl.Buffered(1)),
            pl.BlockSpec((None, tokens, HEAD_PAD), lambda b, h, i: (0, 0, 0), pipeline_mode=pl.Buffered(1)),
            pl.BlockSpec((None, HEAD_PAD, tq), qt_next),
            pl.BlockSpec((None, tokens, HEAD_PAD), k_next),
            pl.BlockSpec((None, V_ROWS, tokens), lambda b, h, i: (b, h, 0)),
        ],
        out_specs=pl.BlockSpec((None, V_DIM, tq), lambda b, h, i: (b, h, i)),
        out_shape=jax.ShapeDtypeStruct((batch, HEADS * V_DIM, tokens), BF16),
        scratch_shapes=[pltpu.VMEM((tokens, tq), F32), pltpu.VMEM((1, tq), F32)],
        compiler_params=pltpu.CompilerParams(
            dimension_semantics=("arbitrary", "arbitrary", "arbitrary"), vmem_limit_bytes=VMEM_LIMIT_BYTES),
        name="attn",
    )(qt, k, qt, k, vt)


def _attn_ctx_kernel(att_ref, qt_ref, k_ref, vt_ref, o_ref):
    del att_ref
    for hd in range(ATTN_HEADS_PER_STEP):
        s = _dot(k_ref[:, hd * HEAD_PAD:(hd + 1) * HEAD_PAD], qt_ref[hd * HEAD_PAD:(hd + 1) * HEAD_PAD, :])
        p = jnp.exp2(s - jnp.max(s, axis=0, keepdims=True)).astype(BF16)
        acc = _dot(vt_ref[hd * V_ROWS:(hd + 1) * V_ROWS, :], p)
        o_ref[hd * V_DIM:(hd + 1) * V_DIM, :] = (acc[:V_DIM] * (1.0 / acc[V_DIM:V_DIM + 1])).astype(BF16)


def _attn_ctx(att, qt, k, vt, *, n_ctx):
    batch, _, tokens = qt.shape
    hb = ATTN_HEADS_PER_STEP
    assert tokens % n_ctx == 0
    c = tokens // n_ctx - 1
    return pl.pallas_call(
        _attn_ctx_kernel,
        grid=(batch, HEADS // hb),
        in_specs=[
            pl.BlockSpec(memory_space=pl.ANY),
            pl.BlockSpec((None, hb * HEAD_PAD, n_ctx), lambda b, g: (b, g, c)),
            pl.BlockSpec((None, n_ctx, hb * HEAD_PAD), lambda b, g: (b, c, g)),
            pl.BlockSpec((None, hb * V_ROWS, n_ctx), lambda b, g: (b, g, c)),
        ],
        out_specs=pl.BlockSpec((None, hb * V_DIM, n_ctx), lambda b, g: (b, g, c)),
        out_shape=jax.ShapeDtypeStruct(att.shape, BF16),
        input_output_aliases={0: 0},
        compiler_params=_params(),
        name="attn_ctx",
    )(att, qt, k, vt)


def _post_kernel(x_ref, xp_ref, xn_ref, att_ref, mod_ref, nw_ref, wcv_ref, cw_ref, wout_ref, wi_ref, wo_ref,
                 fnw_ref, o_ref, xe_ref, u_ref, *, n_lat_blocks, n_blocks, final_norm):
    i = pl.program_id(1)
    t = TOKEN_BLOCK
    x = x_ref[...]
    mod = mod_ref[...]
    nw = nw_ref[...]

    xe_ref[0:HALO, :] = xp_ref[...]
    xe_ref[HALO:HALO + t, :] = x
    xe_ref[HALO + t:, :] = xn_ref[...]
    h = _rmsnorm(xe_ref[...], nw[1:2]) * (1.0 + mod[4:5]) + mod[3:4]
    pj = _dot(h.astype(BF16), wcv_ref[...])
    u_ref[...] = pj[:, CONV_DIM:2 * CONV_DIM] * pj[:, 2 * CONV_DIM:]
    left_ok = jnp.logical_and(i != 0, i != n_lat_blocks)
    right_ok = jnp.logical_and(i != n_lat_blocks - 1, i != n_blocks - 1)
    u_ref[0:HALO, :] = jnp.where(left_ok, u_ref[0:HALO, :], 0.0)
    u_ref[HALO + t:, :] = jnp.where(right_ok, u_ref[HALO + t:, :], 0.0)
    cw = cw_ref[...]
    conv = (u_ref[HALO - 1:HALO - 1 + t, :] * cw[0:1] + u_ref[HALO:HALO + t, :] * cw[1:2]
            + u_ref[HALO + 1:HALO + 1 + t, :] * cw[2:3])
    conv = pj[HALO:HALO + t, :CONV_DIM] * conv

    mix = _dot(jnp.concatenate([att_ref[...].T, conv.astype(BF16)], axis=1), wout_ref[...])
    x1 = x + mod[5:6] * mix
    x2 = _ffn_half_step(x1, nw[2:3], mod[6:7], mod[7:8], mod[8:9], wi_ref, wo_ref)
    if final_norm:
        x2 = _rmsnorm(x2, fnw_ref[...])
    o_ref[...] = x2


def _post(x, att, mods, norm_w, w, final_norm_w, *, layer, n_lat_blocks, n_out_blocks, final_norm):
    batch, tokens, _ = x.shape
    t = TOKEN_BLOCK
    nb = tokens // t
    halo_per_block = t // HALO
    n_halo = tokens // HALO
    kernel = functools.partial(_post_kernel, n_lat_blocks=n_lat_blocks, n_blocks=nb, final_norm=final_norm)
    return pl.pallas_call(
        kernel,
        grid=(batch, n_out_blocks),
        in_specs=[
            pl.BlockSpec((None, t, D_MODEL), lambda b, i: (b, i, 0)),
            pl.BlockSpec((None, HALO, D_MODEL), lambda b, i: (b, jnp.maximum(i * halo_per_block - 1, 0), 0)),
            pl.BlockSpec((None, HALO, D_MODEL),
                         lambda b, i: (b, jnp.minimum((i + 1) * halo_per_block, n_halo - 1), 0)),
            pl.BlockSpec((None, HEADS * V_DIM, t), lambda b, i: (b, 0, i)),
            pl.BlockSpec((None, N_MOD, D_MODEL), lambda b, i: (jnp.where(i >= n_lat_blocks, batch, b), 0, 0)),
            _const_spec((3, D_MODEL)),
            _const_spec((D_MODEL, 3 * CONV_DIM)),
            _const_spec((3, CONV_DIM)),
            _layer_spec((HEADS * V_DIM + CONV_DIM, D_MODEL), layer),
            _layer_spec((D_MODEL, 2 * D_FF), layer),
            _layer_spec((D_FF, D_MODEL), layer),
            _const_spec((1, D_MODEL)),
        ],
        out_specs=pl.BlockSpec((None, t, D_MODEL), lambda b, i: (b, i, 0)),
        out_shape=jax.ShapeDtypeStruct((batch, n_out_blocks * t, D_MODEL), F32),
        scratch_shapes=[pltpu.VMEM((t + 2 * HALO, D_MODEL), F32), pltpu.VMEM((t + 2 * HALO, CONV_DIM), F32)],
        compiler_params=_params(),
        name="post",
    )(x, x, x, att, mods, norm_w, w["wcv"], w["cw"], w["wout"], w["wi2"], w["wo2"], final_norm_w)


def _layer_weights(l, w_in, q_norm_w, kv_norm_w, w_uq, w_ukv, conv_w):
    partner = jnp.array(_ROPE_PARTNER)
    w_in_l = w_in[l]
    kv0 = Q_RANK + KV_RANK
    kpe = w_in_l[:, kv0:kv0 + ROPE]
    wqkv = jnp.concatenate(
        [w_in_l[:, :kv0], kpe, kpe[:, partner], jnp.zeros((D_MODEL, HEAD_PAD - 2 * ROPE), F32)], axis=1)

    uq = w_uq[l].reshape(Q_RANK, HEADS, QK_DIM)
    uq = jnp.concatenate([uq, uq[:, :, NOPE:][:, :, partner]], axis=2)
    wuqt = uq.reshape(Q_RANK, HEADS * HEAD_PAD).T

    ukv = w_ukv[l].reshape(KV_RANK, HEADS, NOPE + V_DIM)
    wk_nope = jnp.concatenate([ukv[:, :, :NOPE], jnp.zeros((KV_RANK, HEADS, HEAD_PAD - NOPE), F32)], axis=2)
    place = np.zeros((KV_RANK, HEADS, HEAD_PAD), np.float32)
    place[np.arange(ROPE), :, NOPE + np.arange(ROPE)] = 1.0
    wk = jnp.concatenate([wk_nope, jnp.asarray(place)], axis=0).reshape(2 * KV_RANK, HEADS * HEAD_PAD)

    wv = jnp.concatenate([ukv[:, :, NOPE:], jnp.zeros((KV_RANK, HEADS, V_ROWS - V_DIM), F32)], axis=2)
    wvt = wv.reshape(KV_RANK, HEADS * V_ROWS).T

    return {
        "wqkv": wqkv.astype(BF16), "wcv": w_in_l[:, kv0 + ROPE:].astype(BF16),
        "qnw": q_norm_w[l].reshape(1, Q_RANK), "kvnw": kv_norm_w[l].reshape(1, KV_RANK),
        "wuqt": wuqt.astype(BF16), "wk": wk.astype(BF16), "wvt": wvt.astype(BF16), "cw": conv_w[l],
    }


def _rope_tables(n_lat, n_ctx):
    t = np.arange(n_lat)
    row = (t // GRID_W).astype(np.float32)
    col = (t % GRID_W).astype(np.float32)
    d_axis = ROPE // 2
    inv = (ROPE_BASE ** (-np.arange(0, d_axis, 2, dtype=np.float32) / d_axis)).astype(np.float32)
    ar = row[:, None] * inv
    ac = col[:, None] * inv
    cr, sr, cc, sc = np.cos(ar), np.sin(ar), np.cos(ac), np.sin(ac)
    cos = np.concatenate([cr, cr, cc, cc], axis=1)
    sin = np.concatenate([-sr, sr, -sc, sc], axis=1)
    cos = np.concatenate([cos, np.ones((n_ctx, ROPE), np.float32)], axis=0)
    sin = np.concatenate([sin, np.zeros((n_ctx, ROPE), np.float32)], axis=0)
    pad = np.zeros((n_lat + n_ctx, HEAD_PAD - ROPE), np.float32)
    tabs = {
        "qcos": (cos * Q_SCALE).T, "qsin": (sin * Q_SCALE).T,
        "kcos": np.concatenate([cos, pad], axis=1), "ksin": np.concatenate([sin, pad], axis=1),
    }
    return {name: jnp.asarray(np.ascontiguousarray(v, dtype=np.float32)) for name, v in tabs.items()}


def kernel(x, c, ctx, c_ctx, w_ada, b_ada, norm_w, w_ffn1_in, w_ffn1_out, w_ffn2_in, w_ffn2_out, w_in, q_norm_w,
           kv_norm_w, w_uq, w_ukv, conv_w, w_out, final_norm_w):
    batch, n_lat, _ = x.shape
    n_ctx = ctx.shape[1]
    depth = w_ada.shape[0]
    assert n_ctx % TOKEN_BLOCK == 0 and n_lat % QUERY_BLOCK == 0 and QUERY_BLOCK % TOKEN_BLOCK == 0
    assert batch + 1 <= 8
    n_lat_blocks = n_lat // TOKEN_BLOCK
    n_blocks = (n_lat + n_ctx) // TOKEN_BLOCK

    cond = jnp.concatenate([c, c_ctx[None, :], jnp.zeros((8 - batch - 1, D_MODEL), F32)], axis=0)
    mods = _ada(cond, w_ada, b_ada).reshape(depth, 8, N_MOD, D_MODEL)
    tabs = _rope_tables(n_lat, n_ctx)
    fnw = final_norm_w.reshape(1, D_MODEL)
    big = {"wi1": w_ffn1_in.astype(BF16), "wo1": w_ffn1_out.astype(BF16), "wi2": w_ffn2_in.astype(BF16),
           "wo2": w_ffn2_out.astype(BF16), "wout": w_out.astype(BF16)}

    xs, xs_ctx = x, ctx
    for l in range(depth):
        last = l == depth - 1
        w = dict(big, **_layer_weights(l, w_in, q_norm_w, kv_norm_w, w_uq, w_ukv, conv_w))
        xs, qt, k, vt = _pre(xs, xs_ctx, mods[l], norm_w[l], w, tabs, l, n_lat_blocks)
        xs_ctx = None
        att = _attn(qt, k, vt, n_q_blocks=n_lat // QUERY_BLOCK)
        if not last:
            att = _attn_ctx(att, qt, k, vt, n_ctx=n_ctx)
        xs = _post(xs, att, mods[l], norm_w[l], w, fnw, layer=l, n_lat_blocks=n_lat_blocks,
                   n_out_blocks=n_lat_blocks if last else n_blocks, final_norm=last)
    return xs
```

```python
import functools
import math

import jax
import jax.numpy as jnp
import numpy as np
from jax import lax
from jax.experimental import pallas as pl
from jax.experimental.pallas import tpu as pltpu

F32 = jnp.float32
BF16 = jnp.bfloat16

D_MODEL = 1024
HEADS = 8
NOPE = 64
ROPE = 32
QK_DIM = NOPE + ROPE
V_DIM = 64
Q_RANK = 256
KV_RANK = 128
CONV_DIM = 512
D_FF = 2816
N_MOD = 9
GRID_W = 64
ROPE_BASE = 10000.0
EPS = 1e-6

HEAD_PAD = 128
V_ROWS = 80
TOKEN_BLOCK = 256
QUERY_BLOCK = 512
KEY_CHUNK = 256
MAX_ROWS = 32
FF_CHUNK = 256
FF_AHEAD = 2
HALO = 8
ATTN_HEADS_PER_STEP = 2
VMEM_LIMIT_BYTES = 56 * 1024 * 1024

Q_SCALE = math.log2(math.e) / math.sqrt(QK_DIM)

_ROPE_PARTNER = tuple(range(8, 16)) + tuple(range(0, 8)) + tuple(range(24, 32)) + tuple(range(16, 24))


def _dot(a, b):
    return jnp.dot(a, b, preferred_element_type=F32)


def _dot_nt(a, b):
    return lax.dot_general(a, b, (((1,), (1,)), ((), ())), preferred_element_type=F32)


def _rmsnorm(x, w):
    r = lax.rsqrt(jnp.mean(x * x, axis=-1, keepdims=True) + EPS)
    return (x * r) * w


def _const_spec(shape):
    return pl.BlockSpec(shape, lambda *_: (0,) * len(shape), pipeline_mode=pl.Buffered(1))


def _layer_spec(shape, layer):
    return pl.BlockSpec((None,) + shape, lambda *_: (layer,) + (0,) * len(shape), pipeline_mode=pl.Buffered(1))


def _params():
    return pltpu.CompilerParams(
        dimension_semantics=("arbitrary", "arbitrary"), vmem_limit_bytes=VMEM_LIMIT_BYTES)


def _flat_params():
    return pltpu.CompilerParams(dimension_semantics=("arbitrary",), vmem_limit_bytes=VMEM_LIMIT_BYTES)


def _ada_kernel(c_ref, w_ref, b_ref, o_ref):
    c = c_ref[...]
    act = (c * (1.0 / (1.0 + jnp.exp(-c)))).astype(BF16)
    o_ref[0] = _dot(act, w_ref[0].astype(BF16)) + b_ref[0]


def _ada(cond, w_ada, b_ada):
    depth = w_ada.shape[0]
    n_out = w_ada.shape[2]
    blk = 1024
    return pl.pallas_call(
        _ada_kernel,
        grid=(depth, n_out // blk),
        in_specs=[
            pl.BlockSpec((8, D_MODEL), lambda l, j: (0, 0)),
            pl.BlockSpec((1, D_MODEL, blk), lambda l, j: (l, 0, j)),
            pl.BlockSpec((1, 1, blk), lambda l, j: (l, 0, j)),
        ],
        out_specs=pl.BlockSpec((1, 8, blk), lambda l, j: (l, 0, j)),
        out_shape=jax.ShapeDtypeStruct((depth, 8, n_out), F32),
        compiler_params=_params(),
        name="ada",
    )(cond, w_ada, b_ada.reshape(depth, 1, n_out))


def _ffn_input(x, norm_w, shift, scale):
    return (_rmsnorm(x, norm_w) * (1.0 + scale) + shift).astype(BF16)


def _ffn_half_step(x, h, gate, wi_ref, wo_ref, side=()):
    side = list(side)
    n_chunks = D_FF // FF_CHUNK

    def gate_up(c):
        lo = c * FF_CHUNK
        return _dot(h, wi_ref[:, lo:lo + FF_CHUNK]), _dot(h, wi_ref[:, D_FF + lo:D_FF + lo + FF_CHUNK])

    pending = [gate_up(c) for c in range(FF_AHEAD)]
    acc = None
    for c in range(n_chunks):
        if c + FF_AHEAD < n_chunks:
            pending.append(gate_up(c + FF_AHEAD))
        g, u = pending.pop(0)
        a = (g * (1.0 / (1.0 + jnp.exp(-g))) * u).astype(BF16)
        y = _dot(a, wo_ref[c * FF_CHUNK:(c + 1) * FF_CHUNK, :])
        acc = y if acc is None else acc + y
        if side and c % 2 == 0:
            side.pop(0)()
    for piece in side:
        piece()
    return x + (0.5 * gate) * acc


def _flat_blocks(n_steps, nb):
    def cur(s):
        sa = jnp.minimum(s, n_steps - 2)
        return sa // nb, sa % nb

    def prev(s):
        sb = jnp.maximum(s - 1, 0)
        return sb // nb, sb % nb

    return cur, prev


def _pre_kernel(*refs, n_steps, nb, n_lat_blocks, split_input):
    s = pl.program_id(0)
    if split_input:
        x_ref, xc_ref, *refs = refs
        x = jnp.where(jnp.minimum(s, n_steps - 2) % nb >= n_lat_blocks, xc_ref[...], x_ref[...])
    else:
        x_ref, *refs = refs
        x = x_ref[...]
    (moda_ref, modb_ref, nw_ref, wi_ref, wo_ref, wqkv_ref, qnw_ref, kvnw_ref, wuqt_ref, wk_ref, wvt_ref,
     qcos_ref, qsin_ref, kcos_ref, ksin_ref, xo_ref, qt_ref, k_ref, vt_ref, x1_ref) = refs
    nw = nw_ref[...]
    mod = moda_ref[...]

    @pl.when(s == 0)
    def _():
        x1_ref[...] = jnp.zeros(x1_ref.shape, F32)

    modb = modb_ref[...]
    x1_prev = x1_ref[...]
    v = {}

    def project():
        h = _rmsnorm(x1_prev, nw[1:2]) * (1.0 + modb[4:5]) + modb[3:4]
        v["pj"] = _dot(h.astype(BF16), wqkv_ref[...])

    def low_rank_norms():
        pj = v["pj"]
        v["nq"] = _rmsnorm(pj[:, :Q_RANK], qnw_ref[...]).astype(BF16)
        v["nkv"] = _rmsnorm(pj[:, Q_RANK:Q_RANK + KV_RANK], kvnw_ref[...]).astype(BF16)
        v["kp"] = pj[:, Q_RANK + KV_RANK:]

    def queries():
        qt = _dot_nt(wuqt_ref[...], v["nq"])
        qcos = qcos_ref[...]
        qsin = qsin_ref[...]
        for hd in range(HEADS):
            r = hd * HEAD_PAD
            qt_ref[r:r + NOPE, :] = (qt[r:r + NOPE] * Q_SCALE).astype(BF16)
            rot = qt[r + NOPE:r + QK_DIM] * qcos + qt[r + QK_DIM:r + HEAD_PAD] * qsin
            qt_ref[r + NOPE:r + QK_DIM, :] = rot.astype(BF16)
            qt_ref[r + QK_DIM:r + HEAD_PAD, :] = jnp.zeros((HEAD_PAD - QK_DIM, qt.shape[1]), BF16)

    def keys():
        kp = v["kp"]
        krot = kp * kcos_ref[...] + pltpu.roll(kp, HEAD_PAD - ROPE, axis=1) * ksin_ref[...]
        k_ref[...] = _dot(jnp.concatenate([v["nkv"], krot.astype(BF16)], axis=1), wk_ref[...]).astype(BF16)

    def values():
        vt = _dot_nt(wvt_ref[...], v["nkv"])
        row = lax.broadcasted_iota(jnp.int32, vt.shape, 0)
        vt_ref[...] = jnp.where(row % V_ROWS == V_DIM, 1.0, vt).astype(BF16)

    x1 = _ffn_half_step(x, _ffn_input(x, nw[0:1], mod[0:1], mod[1:2]), mod[2:3], wi_ref, wo_ref,
                        side=(project, low_rank_norms, queries, keys, values))
    xo_ref[...] = x1
    x1_ref[...] = x1


def _pre(x, x_ctx, mods, norm_w, w, tabs, layer, n_lat_blocks):
    batch = x.shape[0]
    t = TOKEN_BLOCK
    split_input = x_ctx is not None
    tokens = x.shape[1] + (x_ctx.shape[1] if split_input else 0)
    nb = tokens // t
    n_steps = batch * nb + 1
    cur, prev = _flat_blocks(n_steps, nb)

    def mod_row(b, i):
        return (jnp.where(i >= n_lat_blocks, batch, b), 0, 0)

    if split_input:
        x_args = [x, x_ctx]
        x_specs = [
            pl.BlockSpec((None, t, D_MODEL), lambda s: (cur(s)[0], jnp.minimum(cur(s)[1], n_lat_blocks - 1), 0)),
            pl.BlockSpec((None, t, D_MODEL), lambda s: (cur(s)[0], jnp.maximum(cur(s)[1] - n_lat_blocks, 0), 0))]
    else:
        x_args = [x]
        x_specs = [pl.BlockSpec((None, t, D_MODEL), lambda s: (*cur(s), 0))]

    return pl.pallas_call(
        functools.partial(_pre_kernel, n_steps=n_steps, nb=nb, n_lat_blocks=n_lat_blocks,
                          split_input=split_input),
        grid=(n_steps,),
        in_specs=x_specs + [
            pl.BlockSpec((None, N_MOD, D_MODEL), lambda s: mod_row(*cur(s))),
            pl.BlockSpec((None, N_MOD, D_MODEL), lambda s: mod_row(*prev(s))),
            _const_spec((3, D_MODEL)),
            _layer_spec((D_MODEL, 2 * D_FF), layer),
            _layer_spec((D_FF, D_MODEL), layer),
            _const_spec((D_MODEL, 512)),
            _const_spec((1, Q_RANK)),
            _const_spec((1, KV_RANK)),
            _const_spec((HEADS * HEAD_PAD, Q_RANK)),
            _const_spec((2 * KV_RANK, HEADS * HEAD_PAD)),
            _const_spec((HEADS * V_ROWS, KV_RANK)),
            pl.BlockSpec((ROPE, t), lambda s: (0, prev(s)[1])),
            pl.BlockSpec((ROPE, t), lambda s: (0, prev(s)[1])),
            pl.BlockSpec((t, HEAD_PAD), lambda s: (prev(s)[1], 0)),
            pl.BlockSpec((t, HEAD_PAD), lambda s: (prev(s)[1], 0)),
        ],
        out_specs=[
            pl.BlockSpec((None, t, D_MODEL), lambda s: (*cur(s), 0)),
            pl.BlockSpec((None, HEADS * HEAD_PAD, t), lambda s: (prev(s)[0], 0, prev(s)[1])),
            pl.BlockSpec((None, t, HEADS * HEAD_PAD), lambda s: (*prev(s), 0)),
            pl.BlockSpec((None, HEADS * V_ROWS, t), lambda s: (prev(s)[0], 0, prev(s)[1])),
        ],
        out_shape=[
            jax.ShapeDtypeStruct((batch, tokens, D_MODEL), F32),
            jax.ShapeDtypeStruct((batch, HEADS * HEAD_PAD, tokens), BF16),
            jax.ShapeDtypeStruct((batch, tokens, HEADS * HEAD_PAD), BF16),
            jax.ShapeDtypeStruct((batch, HEADS * V_ROWS, tokens), BF16),
        ],
        scratch_shapes=[pltpu.VMEM((t, D_MODEL), F32)],
        compiler_params=_flat_params(),
        name="pre",
    )(*x_args, mods, mods, norm_w, w["wi1"], w["wo1"], w["wqkv"], w["qnw"], w["kvnw"], w["wuqt"], w["wk"],
      w["wvt"], tabs["qcos"], tabs["qsin"], tabs["kcos"], tabs["ksin"])


def _column_max(s, mx):
    parts = [s[r:r + MAX_ROWS] for r in range(0, s.shape[0], MAX_ROWS)]
    if mx is not None:
        parts.append(mx)
    while len(parts) > 1:
        nxt = [jnp.maximum(parts[j], parts[j + 1]) for j in range(0, len(parts) - 1, 2)]
        if len(parts) % 2:
            nxt.append(parts[-1])
        parts = nxt
    return parts[0]


def _scores(k_ref, qt, s_ref, m_ref, n_chunks, key_chunk):
    mx = None
    for c in range(n_chunks):
        rows = slice(c * key_chunk, (c + 1) * key_chunk)
        s = _dot(k_ref[rows, :], qt)
        s_ref[rows, :] = s
        mx = _column_max(s, mx)
    m_ref[...] = jnp.max(mx, axis=0, keepdims=True)


def _attn_kernel(qt0_ref, k0_ref, qtn_ref, kn_ref, vt_ref, o_ref, s_ref, m_ref, *, n_keys, key_chunk):
    n_chunks = n_keys // key_chunk
    tq = qtn_ref.shape[1]
    first = jnp.logical_and(jnp.logical_and(pl.program_id(0) == 0, pl.program_id(1) == 0),
                            pl.program_id(2) == 0)

    @pl.when(first)
    def _():
        _scores(k0_ref, qt0_ref[...], s_ref, m_ref, n_chunks, key_chunk)

    m = m_ref[...]
    qtn = qtn_ref[...]
    acc = jnp.zeros((V_ROWS, tq), F32)
    mx = None
    for c in range(n_chunks):
        rows = slice(c * key_chunk, (c + 1) * key_chunk)
        p = jnp.exp2(s_ref[rows, :] - m).astype(BF16)
        acc = acc + _dot(vt_ref[:, rows], p)
        s = _dot(kn_ref[rows, :], qtn)
        s_ref[rows, :] = s
        mx = _column_max(s, mx)
    m_ref[...] = jnp.max(mx, axis=0, keepdims=True)
    o_ref[...] = (acc[:V_DIM] * (1.0 / acc[V_DIM:V_DIM + 1])).astype(BF16)


def _attn(qt, k, vt, *, n_q_blocks):
    batch, _, tokens = qt.shape
    tq = QUERY_BLOCK
    assert tokens % KEY_CHUNK == 0

    def nxt(b, h, i):
        wrap_i = i == n_q_blocks - 1
        wrap_h = jnp.logical_and(wrap_i, h == HEADS - 1)
        last = jnp.logical_and(wrap_h, b == batch - 1)
        i2 = jnp.where(wrap_i, 0, i + 1)
        h2 = jnp.where(wrap_i, jnp.where(h == HEADS - 1, 0, h + 1), h)
        b2 = jnp.where(wrap_h, b + 1, b)
        return jnp.where(last, b, b2), jnp.where(last, h, h2), jnp.where(last, i, i2)

    def qt_next(b, h, i):
        b2, h2, i2 = nxt(b, h, i)
        return (b2, h2, i2)

    def k_next(b, h, i):
        b2, h2, _ = nxt(b, h, i)
        return (b2, 0, h2)

    kernel = functools.partial(_attn_kernel, n_keys=tokens, key_chunk=KEY_CHUNK)
    return pl.pallas_call(
        kernel,
        grid=(batch, HEADS, n_q_blocks),
        in_specs=[
            pl.BlockSpec((None, HEAD_PAD, tq), lambda b, h, i: (0, 0, 0), pipeline_mode=pl.Buffered(1)),
            pl.BlockSpec((None, tokens, HEAD_PAD), lambda b, h, i: (0, 0, 0), pipeline_mode=pl.Buffered(1)),
            pl.BlockSpec((None, HEAD_PAD, tq), qt_next),
            pl.BlockSpec((None, tokens, HEAD_PAD), k_next),
            pl.BlockSpec((None, V_ROWS, tokens), lambda b, h, i: (b, h, 0)),
        ],
        out_specs=pl.BlockSpec((None, V_DIM, tq), lambda b, h, i: (b, h, i)),
        out_shape=jax.ShapeDtypeStruct((batch, HEADS * V_DIM, n_q_blocks * tq), BF16),
        scratch_shapes=[pltpu.VMEM((tokens, tq), F32), pltpu.VMEM((1, tq), F32)],
        compiler_params=pltpu.CompilerParams(
            dimension_semantics=("arbitrary", "arbitrary", "arbitrary"), vmem_limit_bytes=VMEM_LIMIT_BYTES),
        name="attn",
    )(qt, k, qt, k, vt)


def _attn_ctx_kernel(qt_ref, k_ref, vt_ref, o_ref):
    for hd in range(ATTN_HEADS_PER_STEP):
        s = _dot(k_ref[:, hd * HEAD_PAD:(hd + 1) * HEAD_PAD], qt_ref[hd * HEAD_PAD:(hd + 1) * HEAD_PAD, :])
        p = jnp.exp2(s - jnp.max(s, axis=0, keepdims=True)).astype(BF16)
        acc = _dot(vt_ref[hd * V_ROWS:(hd + 1) * V_ROWS, :], p)
        o_ref[hd * V_DIM:(hd + 1) * V_DIM, :] = (acc[:V_DIM] * (1.0 / acc[V_DIM:V_DIM + 1])).astype(BF16)


def _attn_ctx(qt, k, vt, *, n_ctx):
    batch, _, tokens = qt.shape
    hb = ATTN_HEADS_PER_STEP
    assert tokens % n_ctx == 0
    c = tokens // n_ctx - 1
    return pl.pallas_call(
        _attn_ctx_kernel,
        grid=(batch, HEADS // hb),
        in_specs=[
            pl.BlockSpec((None, hb * HEAD_PAD, n_ctx), lambda b, g: (b, g, c)),
            pl.BlockSpec((None, n_ctx, hb * HEAD_PAD), lambda b, g: (b, c, g)),
            pl.BlockSpec((None, hb * V_ROWS, n_ctx), lambda b, g: (b, g, c)),
        ],
        out_specs=pl.BlockSpec((None, hb * V_DIM, n_ctx), lambda b, g: (b, g, 0)),
        out_shape=jax.ShapeDtypeStruct((batch, HEADS * V_DIM, n_ctx), BF16),
        compiler_params=_params(),
        name="attn_ctx",
    )(qt, k, vt)


def _post_kernel(*refs, n_steps, nb, n_lat_blocks, n_blocks, has_ctx, final_norm):
    s = pl.program_id(0)
    if has_ctx:
        x_ref, xp_ref, xn_ref, att_ref, attc_ref, *refs = refs
    else:
        x_ref, xp_ref, xn_ref, att_ref, *refs = refs
    (moda_ref, modb_ref, nw_ref, wcv_ref, cw_ref, wout_ref, wi_ref, wo_ref, fnw_ref,
     o_ref, xe_ref, u_ref, x1_ref, h_ref) = refs
    i = jnp.minimum(s, n_steps - 2) % nb
    t = TOKEN_BLOCK
    nw = nw_ref[...]

    @pl.when(s == 0)
    def _():
        x1_ref[...] = jnp.zeros(x1_ref.shape, F32)
        h_ref[...] = jnp.zeros(h_ref.shape, BF16)

    moda = moda_ref[...]
    x = x_ref[...]
    v = {}

    def conv_project():
        xe_ref[0:HALO, :] = xp_ref[...]
        xe_ref[HALO:HALO + t, :] = x
        xe_ref[HALO + t:, :] = xn_ref[...]
        h = _rmsnorm(xe_ref[...], nw[1:2]) * (1.0 + moda[4:5]) + moda[3:4]
        v["pj"] = _dot(h.astype(BF16), wcv_ref[...])

    def gated_conv():
        pj = v["pj"]
        u_ref[...] = pj[:, CONV_DIM:2 * CONV_DIM] * pj[:, 2 * CONV_DIM:]
        left_ok = jnp.logical_and(i != 0, i != n_lat_blocks)
        right_ok = jnp.logical_and(i != n_lat_blocks - 1, i != n_blocks - 1)
        u_ref[0:HALO, :] = jnp.where(left_ok, u_ref[0:HALO, :], 0.0)
        u_ref[HALO + t:, :] = jnp.where(right_ok, u_ref[HALO + t:, :], 0.0)
        cw = cw_ref[...]
        conv = (u_ref[HALO - 1:HALO - 1 + t, :] * cw[0:1] + u_ref[HALO:HALO + t, :] * cw[1:2]
                + u_ref[HALO + 1:HALO + 1 + t, :] * cw[2:3])
        v["conv"] = (pj[HALO:HALO + t, :CONV_DIM] * conv).astype(BF16)

    def mix_out():
        att = att_ref[...]
        if has_ctx:
            att = jnp.where(i >= n_lat_blocks, attc_ref[...], att)
        mix = _dot(jnp.concatenate([att.T, v["conv"]], axis=1), wout_ref[...])
        x1 = x + moda[5:6] * mix
        v["x1"] = x1
        v["h"] = _ffn_input(x1, nw[2:3], moda[6:7], moda[7:8])

    x2 = _ffn_half_step(x1_ref[...], h_ref[...], modb_ref[...][8:9], wi_ref, wo_ref,
                        side=(conv_project, gated_conv, mix_out))
    if final_norm:
        x2 = _rmsnorm(x2, fnw_ref[...])
    o_ref[...] = x2
    x1_ref[...] = v["x1"]
    h_ref[...] = v["h"]


def _post(x, att, att_ctx, mods, norm_w, w, final_norm_w, *, layer, n_lat_blocks, n_out_blocks, final_norm):
    batch, tokens, _ = x.shape
    t = TOKEN_BLOCK
    halo_per_block = t // HALO
    n_halo = tokens // HALO
    nb = n_out_blocks
    n_steps = batch * nb + 1
    cur, prev = _flat_blocks(n_steps, nb)
    has_ctx = att_ctx is not None

    def mod_row(b, i):
        return (jnp.where(i >= n_lat_blocks, batch, b), 0, 0)

    att_args = [att]
    att_specs = [pl.BlockSpec((None, HEADS * V_DIM, t),
                              lambda s: (cur(s)[0], 0, jnp.minimum(cur(s)[1], n_lat_blocks - 1)))]
    if has_ctx:
        att_args.append(att_ctx)
        att_specs.append(pl.BlockSpec((None, HEADS * V_DIM, t),
                                      lambda s: (cur(s)[0], 0, jnp.maximum(cur(s)[1] - n_lat_blocks, 0))))
    kernel = functools.partial(_post_kernel, n_steps=n_steps, nb=nb, n_lat_blocks=n_lat_blocks,
                               n_blocks=tokens // t, has_ctx=has_ctx, final_norm=final_norm)
    return pl.pallas_call(
        kernel,
        grid=(n_steps,),
        in_specs=[
            pl.BlockSpec((None, t, D_MODEL), lambda s: (*cur(s), 0)),
            pl.BlockSpec((None, HALO, D_MODEL),
                         lambda s: (cur(s)[0], jnp.maximum(cur(s)[1] * halo_per_block - 1, 0), 0)),
            pl.BlockSpec((None, HALO, D_MODEL),
                         lambda s: (cur(s)[0], jnp.minimum((cur(s)[1] + 1) * halo_per_block, n_halo - 1), 0)),
        ] + att_specs + [
            pl.BlockSpec((None, N_MOD, D_MODEL), lambda s: mod_row(*cur(s))),
            pl.BlockSpec((None, N_MOD, D_MODEL), lambda s: mod_row(*prev(s))),
            _const_spec((3, D_MODEL)),
            _const_spec((D_MODEL, 3 * CONV_DIM)),
            _const_spec((3, CONV_DIM)),
            _layer_spec((HEADS * V_DIM + CONV_DIM, D_MODEL), layer),
            _layer_spec((D_MODEL, 2 * D_FF), layer),
            _layer_spec((D_FF, D_MODEL), layer),
            _const_spec((1, D_MODEL)),
        ],
        out_specs=pl.BlockSpec((None, t, D_MODEL), lambda s: (*prev(s), 0)),
        out_shape=jax.ShapeDtypeStruct((batch, n_out_blocks * t, D_MODEL), F32),
        scratch_shapes=[pltpu.VMEM((t + 2 * HALO, D_MODEL), F32), pltpu.VMEM((t + 2 * HALO, CONV_DIM), F32),
                        pltpu.VMEM((t, D_MODEL), F32), pltpu.VMEM((t, D_MODEL), BF16)],
        compiler_params=_flat_params(),
        name="post",
    )(x, x, x, *att_args, mods, mods, norm_w, w["wcv"], w["cw"], w["wout"], w["wi2"], w["wo2"], final_norm_w)


def _layer_weights(l, w_in, q_norm_w, kv_norm_w, w_uq, w_ukv, conv_w):
    partner = jnp.array(_ROPE_PARTNER)
    w_in_l = w_in[l]
    kv0 = Q_RANK + KV_RANK
    kpe = w_in_l[:, kv0:kv0 + ROPE]
    wqkv = jnp.concatenate(
        [w_in_l[:, :kv0], kpe, kpe[:, partner], jnp.zeros((D_MODEL, HEAD_PAD - 2 * ROPE), F32)], axis=1)

    uq = w_uq[l].reshape(Q_RANK, HEADS, QK_DIM)
    uq = jnp.concatenate([uq, uq[:, :, NOPE:][:, :, partner]], axis=2)
    wuqt = uq.reshape(Q_RANK, HEADS * HEAD_PAD).T

    ukv = w_ukv[l].reshape(KV_RANK, HEADS, NOPE + V_DIM)
    wk_nope = jnp.concatenate([ukv[:, :, :NOPE], jnp.zeros((KV_RANK, HEADS, HEAD_PAD - NOPE), F32)], axis=2)
    place = np.zeros((KV_RANK, HEADS, HEAD_PAD), np.float32)
    place[np.arange(ROPE), :, NOPE + np.arange(ROPE)] = 1.0
    wk = jnp.concatenate([wk_nope, jnp.asarray(place)], axis=0).reshape(2 * KV_RANK, HEADS * HEAD_PAD)

    wv = jnp.concatenate([ukv[:, :, NOPE:], jnp.zeros((KV_RANK, HEADS, V_ROWS - V_DIM), F32)], axis=2)
    wvt = wv.reshape(KV_RANK, HEADS * V_ROWS).T

    return {
        "wqkv": wqkv.astype(BF16), "wcv": w_in_l[:, kv0 + ROPE:].astype(BF16),
        "qnw": q_norm_w[l].reshape(1, Q_RANK), "kvnw": kv_norm_w[l].reshape(1, KV_RANK),
        "wuqt": wuqt.astype(BF16), "wk": wk.astype(BF16), "wvt": wvt.astype(BF16), "cw": conv_w[l],
    }


def _rope_tables(n_lat, n_ctx):
    t = np.arange(n_lat)
    row = (t // GRID_W).astype(np.float32)
    col = (t % GRID_W).astype(np.float32)
    d_axis = ROPE // 2
    inv = (ROPE_BASE ** (-np.arange(0, d_axis, 2, dtype=np.float32) / d_axis)).astype(np.float32)
    ar = row[:, None] * inv
    ac = col[:, None] * inv
    cr, sr, cc, sc = np.cos(ar), np.sin(ar), np.cos(ac), np.sin(ac)
    cos = np.concatenate([cr, cr, cc, cc], axis=1)
    sin = np.concatenate([-sr, sr, -sc, sc], axis=1)
    cos = np.concatenate([cos, np.ones((n_ctx, ROPE), np.float32)], axis=0)
    sin = np.concatenate([sin, np.zeros((n_ctx, ROPE), np.float32)], axis=0)
    pad = np.zeros((n_lat + n_ctx, HEAD_PAD - ROPE), np.float32)
    tabs = {
        "qcos": (cos * Q_SCALE).T, "qsin": (sin * Q_SCALE).T,
        "kcos": np.concatenate([cos, pad], axis=1), "ksin": np.concatenate([sin, pad], axis=1),
    }
    return {name: jnp.asarray(np.ascontiguousarray(v, dtype=np.float32)) for name, v in tabs.items()}


def kernel(x, c, ctx, c_ctx, w_ada, b_ada, norm_w, w_ffn1_in, w_ffn1_out, w_ffn2_in, w_ffn2_out, w_in, q_norm_w,
           kv_norm_w, w_uq, w_ukv, conv_w, w_out, final_norm_w):
    batch, n_lat, _ = x.shape
    n_ctx = ctx.shape[1]
    depth = w_ada.shape[0]
    assert n_ctx % TOKEN_BLOCK == 0 and n_lat % QUERY_BLOCK == 0 and QUERY_BLOCK % TOKEN_BLOCK == 0
    assert batch + 1 <= 8
    n_lat_blocks = n_lat // TOKEN_BLOCK
    n_blocks = (n_lat + n_ctx) // TOKEN_BLOCK

    cond = jnp.concatenate([c, c_ctx[None, :], jnp.zeros((8 - batch - 1, D_MODEL), F32)], axis=0)
    mods = _ada(cond, w_ada, b_ada).reshape(depth, 8, N_MOD, D_MODEL)
    tabs = _rope_tables(n_lat, n_ctx)
    fnw = final_norm_w.reshape(1, D_MODEL)
    big = {"wi1": w_ffn1_in.astype(BF16), "wo1": w_ffn1_out.astype(BF16), "wi2": w_ffn2_in.astype(BF16),
           "wo2": w_ffn2_out.astype(BF16), "wout": w_out.astype(BF16)}

    xs, xs_ctx = x, ctx
    for l in range(depth):
        last = l == depth - 1
        w = dict(big, **_layer_weights(l, w_in, q_norm_w, kv_norm_w, w_uq, w_ukv, conv_w))
        xs, qt, k, vt = _pre(xs, xs_ctx, mods[l], norm_w[l], w, tabs, l, n_lat_blocks)
        xs_ctx = None
        att = _attn(qt, k, vt, n_q_blocks=n_lat // QUERY_BLOCK)
        att_ctx = None if last else _attn_ctx(qt, k, vt, n_ctx=n_ctx)
        xs = _post(xs, att, att_ctx, mods[l], norm_w[l], w, fnw, layer=l, n_lat_blocks=n_lat_blocks,
                   n_out_blocks=n_lat_blocks if last else n_blocks, final_norm=last)
    return xs
```

```python
import functools
import math

import jax
import jax.numpy as jnp
import numpy as np
from jax import lax
from jax.experimental import pallas as pl
from jax.experimental.pallas import tpu as pltpu

F32 = jnp.float32
BF16 = jnp.bfloat16

D_MODEL = 1024
HEADS = 8
NOPE = 64
ROPE = 32
QK_DIM = NOPE + ROPE
V_DIM = 64
Q_RANK = 256
KV_RANK = 128
CONV_DIM = 512
D_FF = 2816
N_MOD = 9
GRID_W = 64
ROPE_BASE = 10000.0
EPS = 1e-6

HEAD_PAD = 128
V_ROWS = 80
TOKEN_BLOCK = 256
GROUP = 3
QUERY_BLOCK = 512
KEY_CHUNK = 256
MAX_ROWS = 32
FF_CHUNK = 256
FF_AHEAD = 2
HALO = 8
ATTN_HEADS_PER_STEP = 2
VMEM_LIMIT_BYTES = 56 * 1024 * 1024

Q_SCALE = math.log2(math.e) / math.sqrt(QK_DIM)

_ROPE_PARTNER = tuple(range(8, 16)) + tuple(range(0, 8)) + tuple(range(24, 32)) + tuple(range(16, 24))


def _dot(a, b):
    return jnp.dot(a, b, preferred_element_type=F32)


def _dot_nt(a, b):
    return lax.dot_general(a, b, (((1,), (1,)), ((), ())), preferred_element_type=F32)


def _rmsnorm(x, w):
    r = lax.rsqrt(jnp.mean(x * x, axis=-1, keepdims=True) + EPS)
    return (x * r) * w


def _const_spec(shape):
    return pl.BlockSpec(shape, lambda *_: (0,) * len(shape), pipeline_mode=pl.Buffered(1))


def _layer_spec(shape, layer):
    return pl.BlockSpec((None,) + shape, lambda *_: (layer,) + (0,) * len(shape), pipeline_mode=pl.Buffered(1))


def _params():
    return pltpu.CompilerParams(
        dimension_semantics=("arbitrary", "arbitrary"), vmem_limit_bytes=VMEM_LIMIT_BYTES)


def _ada_kernel(c_ref, w_ref, b_ref, o_ref):
    c = c_ref[...]
    act = (c * (1.0 / (1.0 + jnp.exp(-c)))).astype(BF16)
    o_ref[0] = _dot(act, w_ref[0].astype(BF16)) + b_ref[0]


def _ada(cond, w_ada, b_ada):
    depth = w_ada.shape[0]
    n_out = w_ada.shape[2]
    blk = 1024
    return pl.pallas_call(
        _ada_kernel,
        grid=(depth, n_out // blk),
        in_specs=[
            pl.BlockSpec((8, D_MODEL), lambda l, j: (0, 0)),
            pl.BlockSpec((1, D_MODEL, blk), lambda l, j: (l, 0, j)),
            pl.BlockSpec((1, 1, blk), lambda l, j: (l, 0, j)),
        ],
        out_specs=pl.BlockSpec((1, 8, blk), lambda l, j: (l, 0, j)),
        out_shape=jax.ShapeDtypeStruct((depth, 8, n_out), F32),
        compiler_params=_params(),
        name="ada",
    )(cond, w_ada, b_ada.reshape(depth, 1, n_out))


def _ffn_input(x, norm_w, shift, scale):
    return (_rmsnorm(x, norm_w) * (1.0 + scale) + shift).astype(BF16)


def _ffn_half_step(x, h, gate, wi_ref, wo_ref, side=()):
    side = list(side)
    n_chunks = D_FF // FF_CHUNK

    def gate_up(c):
        lo = c * FF_CHUNK
        return _dot(h, wi_ref[:, lo:lo + FF_CHUNK]), _dot(h, wi_ref[:, D_FF + lo:D_FF + lo + FF_CHUNK])

    pending = [gate_up(c) for c in range(FF_AHEAD)]
    acc = None
    for c in range(n_chunks):
        if c + FF_AHEAD < n_chunks:
            pending.append(gate_up(c + FF_AHEAD))
        g, u = pending.pop(0)
        a = (g * (1.0 / (1.0 + jnp.exp(-g))) * u).astype(BF16)
        y = _dot(a, wo_ref[c * FF_CHUNK:(c + 1) * FF_CHUNK, :])
        acc = y if acc is None else acc + y
        if side and c % 2 == 0:
            side.pop(0)()
    for piece in side:
        piece()
    return x + (0.5 * gate) * acc


def _mla_pieces(x1, mod, nw, rows, refs):
    (wqkv_ref, qnw_ref, kvnw_ref, wuqt_ref, wk_ref, wvt_ref, qcos_ref, qsin_ref, kcos_ref, ksin_ref,
     qt_ref, k_ref, vt_ref) = refs
    v = {}

    def project():
        h = _rmsnorm(x1, nw[1:2]) * (1.0 + mod[4:5]) + mod[3:4]
        v["pj"] = _dot(h.astype(BF16), wqkv_ref[...])

    def low_rank_norms():
        pj = v["pj"]
        v["nq"] = _rmsnorm(pj[:, :Q_RANK], qnw_ref[...]).astype(BF16)
        v["nkv"] = _rmsnorm(pj[:, Q_RANK:Q_RANK + KV_RANK], kvnw_ref[...]).astype(BF16)
        v["kp"] = pj[:, Q_RANK + KV_RANK:]

    def queries():
        qt = _dot_nt(wuqt_ref[...], v["nq"])
        qcos = qcos_ref[:, rows]
        qsin = qsin_ref[:, rows]
        for hd in range(HEADS):
            r = hd * HEAD_PAD
            qt_ref[r:r + NOPE, rows] = (qt[r:r + NOPE] * Q_SCALE).astype(BF16)
            rot = qt[r + NOPE:r + QK_DIM] * qcos + qt[r + QK_DIM:r + HEAD_PAD] * qsin
            qt_ref[r + NOPE:r + QK_DIM, rows] = rot.astype(BF16)
            qt_ref[r + QK_DIM:r + HEAD_PAD, rows] = jnp.zeros((HEAD_PAD - QK_DIM, qt.shape[1]), BF16)

    def keys():
        kp = v["kp"]
        krot = kp * kcos_ref[rows, :] + pltpu.roll(kp, HEAD_PAD - ROPE, axis=1) * ksin_ref[rows, :]
        k_ref[rows, :] = _dot(jnp.concatenate([v["nkv"], krot.astype(BF16)], axis=1), wk_ref[...]).astype(BF16)

    def values():
        vt = _dot_nt(wvt_ref[...], v["nkv"])
        row = lax.broadcasted_iota(jnp.int32, vt.shape, 0)
        vt_ref[:, rows] = jnp.where(row % V_ROWS == V_DIM, 1.0, vt).astype(BF16)

    return (project, low_rank_norms, queries, keys, values)


def _pre_kernel(*refs, n_groups, split_input):
    t = TOKEN_BLOCK
    if split_input:
        *x_refs, xc_ref = refs[:GROUP + 1]
        refs = refs[GROUP + 1:]
        xs = [r[...] for r in x_refs]
        xs[-1] = jnp.where(pl.program_id(1) == n_groups - 1, xc_ref[...], xs[-1])
    else:
        x_ref, *refs = refs
        xs = [x_ref[j * t:(j + 1) * t, :] for j in range(GROUP)]
    mod_refs, refs = refs[:GROUP], refs[GROUP:]
    nw_ref, wi_ref, wo_ref, *side_refs, xo_ref, qt_ref, k_ref, vt_ref = refs
    side_refs = (*side_refs, qt_ref, k_ref, vt_ref)
    nw = nw_ref[...]

    sides = ()
    for j in range(GROUP):
        rows = slice(j * t, (j + 1) * t)
        mod = mod_refs[j][...]
        x1 = _ffn_half_step(xs[j], _ffn_input(xs[j], nw[0:1], mod[0:1], mod[1:2]), mod[2:3], wi_ref, wo_ref,
                            side=sides)
        xo_ref[rows, :] = x1
        sides = _mla_pieces(x1, mod, nw, rows, side_refs)
    for piece in sides:
        piece()


def _pre(x, x_ctx, mods, norm_w, w, tabs, layer, n_lat_blocks):
    batch = x.shape[0]
    t = TOKEN_BLOCK
    tg = t * GROUP
    split_input = x_ctx is not None
    tokens = x.shape[1] + (x_ctx.shape[1] if split_input else 0)
    assert tokens % tg == 0
    ng = tokens // tg

    if split_input:
        assert x_ctx.shape[1] == t
        x_args = [x] * GROUP + [x_ctx]
        x_specs = [pl.BlockSpec((None, t, D_MODEL),
                                lambda b, i, j=j: (b, jnp.minimum(i * GROUP + j, n_lat_blocks - 1), 0))
                   for j in range(GROUP)]
        x_specs.append(pl.BlockSpec((None, t, D_MODEL), lambda b, i: (b, 0, 0)))
    else:
        x_args = [x]
        x_specs = [pl.BlockSpec((None, tg, D_MODEL), lambda b, i: (b, i, 0))]
    mod_specs = [pl.BlockSpec((None, N_MOD, D_MODEL),
                              lambda b, i, j=j: (jnp.where(i * GROUP + j >= n_lat_blocks, batch, b), 0, 0))
                 for j in range(GROUP)]

    return pl.pallas_call(
        functools.partial(_pre_kernel, n_groups=ng, split_input=split_input),
        grid=(batch, ng),
        in_specs=x_specs + mod_specs + [
            _const_spec((3, D_MODEL)),
            _layer_spec((D_MODEL, 2 * D_FF), layer),
            _layer_spec((D_FF, D_MODEL), layer),
            _const_spec((D_MODEL, 512)),
            _const_spec((1, Q_RANK)),
            _const_spec((1, KV_RANK)),
            _const_spec((HEADS * HEAD_PAD, Q_RANK)),
            _const_spec((2 * KV_RANK, HEADS * HEAD_PAD)),
            _const_spec((HEADS * V_ROWS, KV_RANK)),
            pl.BlockSpec((ROPE, tg), lambda b, i: (0, i)),
            pl.BlockSpec((ROPE, tg), lambda b, i: (0, i)),
            pl.BlockSpec((tg, HEAD_PAD), lambda b, i: (i, 0)),
            pl.BlockSpec((tg, HEAD_PAD), lambda b, i: (i, 0)),
        ],
        out_specs=[
            pl.BlockSpec((None, tg, D_MODEL), lambda b, i: (b, i, 0)),
            pl.BlockSpec((None, HEADS * HEAD_PAD, tg), lambda b, i: (b, 0, i)),
            pl.BlockSpec((None, tg, HEADS * HEAD_PAD), lambda b, i: (b, i, 0)),
            pl.BlockSpec((None, HEADS * V_ROWS, tg), lambda b, i: (b, 0, i)),
        ],
        out_shape=[
            jax.ShapeDtypeStruct((batch, tokens, D_MODEL), F32),
            jax.ShapeDtypeStruct((batch, HEADS * HEAD_PAD, tokens), BF16),
            jax.ShapeDtypeStruct((batch, tokens, HEADS * HEAD_PAD), BF16),
            jax.ShapeDtypeStruct((batch, HEADS * V_ROWS, tokens), BF16),
        ],
        compiler_params=_params(),
        name="pre",
    )(*x_args, *[mods] * GROUP, norm_w, w["wi1"], w["wo1"], w["wqkv"], w["qnw"], w["kvnw"], w["wuqt"], w["wk"],
      w["wvt"], tabs["qcos"], tabs["qsin"], tabs["kcos"], tabs["ksin"])


def _column_max(s, mx):
    parts = [s[r:r + MAX_ROWS] for r in range(0, s.shape[0], MAX_ROWS)]
    if mx is not None:
        parts.append(mx)
    while len(parts) > 1:
        nxt = [jnp.maximum(parts[j], parts[j + 1]) for j in range(0, len(parts) - 1, 2)]
        if len(parts) % 2:
            nxt.append(parts[-1])
        parts = nxt
    return parts[0]


def _scores(k_ref, qt, s_ref, m_ref, n_chunks, key_chunk):
    mx = None
    for c in range(n_chunks):
        rows = slice(c * key_chunk, (c + 1) * key_chunk)
        s = _dot(k_ref[rows, :], qt)
        s_ref[rows, :] = s
        mx = _column_max(s, mx)
    m_ref[...] = jnp.max(mx, axis=0, keepdims=True)


def _attn_kernel(qt0_ref, k0_ref, qtn_ref, kn_ref, vt_ref, o_ref, s_ref, m_ref, *, n_keys, key_chunk):
    n_chunks = n_keys // key_chunk
    tq = qtn_ref.shape[1]
    first = jnp.logical_and(jnp.logical_and(pl.program_id(0) == 0, pl.program_id(1) == 0),
                            pl.program_id(2) == 0)

    @pl.when(first)
    def _():
        _scores(k0_ref, qt0_ref[...], s_ref, m_ref, n_chunks, key_chunk)

    m = m_ref[...]
    qtn = qtn_ref[...]
    acc = jnp.zeros((V_ROWS, tq), F32)
    mx = None
    for c in range(n_chunks):
        rows = slice(c * key_chunk, (c + 1) * key_chunk)
        p = jnp.exp2(s_ref[rows, :] - m).astype(BF16)
        acc = acc + _dot(vt_ref[:, rows], p)
        s = _dot(kn_ref[rows, :], qtn)
        s_ref[rows, :] = s
        mx = _column_max(s, mx)
    m_ref[...] = jnp.max(mx, axis=0, keepdims=True)
    o_ref[...] = (acc[:V_DIM] * (1.0 / acc[V_DIM:V_DIM + 1])).astype(BF16)


def _attn(qt, k, vt, *, n_q_blocks):
    batch, _, tokens = qt.shape
    tq = QUERY_BLOCK
    assert tokens % KEY_CHUNK == 0

    def nxt(b, h, i):
        wrap_i = i == n_q_blocks - 1
        wrap_h = jnp.logical_and(wrap_i, h == HEADS - 1)
        last = jnp.logical_and(wrap_h, b == batch - 1)
        i2 = jnp.where(wrap_i, 0, i + 1)
        h2 = jnp.where(wrap_i, jnp.where(h == HEADS - 1, 0, h + 1), h)
        b2 = jnp.where(wrap_h, b + 1, b)
        return jnp.where(last, b, b2), jnp.where(last, h, h2), jnp.where(last, i, i2)

    def qt_next(b, h, i):
        b2, h2, i2 = nxt(b, h, i)
        return (b2, h2, i2)

    def k_next(b, h, i):
        b2, h2, _ = nxt(b, h, i)
        return (b2, 0, h2)

    kernel = functools.partial(_attn_kernel, n_keys=tokens, key_chunk=KEY_CHUNK)
    return pl.pallas_call(
        kernel,
        grid=(batch, HEADS, n_q_blocks),
        in_specs=[
            pl.BlockSpec((None, HEAD_PAD, tq), lambda b, h, i: (0, 0, 0), pipeline_mode=pl.Buffered(1)),
            pl.BlockSpec((None, tokens, HEAD_PAD), lambda b, h, i: (0, 0, 0), pipeline_mode=pl.Buffered(1)),
            pl.BlockSpec((None, HEAD_PAD, tq), qt_next),
            pl.BlockSpec((None, tokens, HEAD_PAD), k_next),
            pl.BlockSpec((None, V_ROWS, tokens), lambda b, h, i: (b, h, 0)),
        ],
        out_specs=pl.BlockSpec((None, V_DIM, tq), lambda b, h, i: (b, h, i)),
        out_shape=jax.ShapeDtypeStruct((batch, HEADS * V_DIM, n_q_blocks * tq), BF16),
        scratch_shapes=[pltpu.VMEM((tokens, tq), F32), pltpu.VMEM((1, tq), F32)],
        compiler_params=pltpu.CompilerParams(
            dimension_semantics=("arbitrary", "arbitrary", "arbitrary"), vmem_limit_bytes=VMEM_LIMIT_BYTES),
        name="attn",
    )(qt, k, qt, k, vt)


def _attn_ctx_kernel(qt_ref, k_ref, vt_ref, o_ref):
    for hd in range(ATTN_HEADS_PER_STEP):
        s = _dot(k_ref[:, hd * HEAD_PAD:(hd + 1) * HEAD_PAD], qt_ref[hd * HEAD_PAD:(hd + 1) * HEAD_PAD, :])
        p = jnp.exp2(s - jnp.max(s, axis=0, keepdims=True)).astype(BF16)
        acc = _dot(vt_ref[hd * V_ROWS:(hd + 1) * V_ROWS, :], p)
        o_ref[hd * V_DIM:(hd + 1) * V_DIM, :] = (acc[:V_DIM] * (1.0 / acc[V_DIM:V_DIM + 1])).astype(BF16)


def _attn_ctx(qt, k, vt, *, n_ctx):
    batch, _, tokens = qt.shape
    hb = ATTN_HEADS_PER_STEP
    assert tokens % n_ctx == 0
    c = tokens // n_ctx - 1
    return pl.pallas_call(
        _attn_ctx_kernel,
        grid=(batch, HEADS // hb),
        in_specs=[
            pl.BlockSpec((None, hb * HEAD_PAD, n_ctx), lambda b, g: (b, g, c)),
            pl.BlockSpec((None, n_ctx, hb * HEAD_PAD), lambda b, g: (b, c, g)),
            pl.BlockSpec((None, hb * V_ROWS, n_ctx), lambda b, g: (b, g, c)),
        ],
        out_specs=pl.BlockSpec((None, hb * V_DIM, n_ctx), lambda b, g: (b, g, 0)),
        out_shape=jax.ShapeDtypeStruct((batch, HEADS * V_DIM, n_ctx), BF16),
        compiler_params=_params(),
        name="attn_ctx",
    )(qt, k, vt)


def _post_kernel(*refs, group, n_lat_blocks, n_blocks, has_ctx, final_norm):
    t = TOKEN_BLOCK
    i = pl.program_id(1)
    x_ref, xp_ref, xn_ref, *refs = refs
    att_refs, refs = refs[:group], refs[group:]
    if has_ctx:
        attc_ref, *refs = refs
    mod_refs, refs = refs[:group], refs[group:]
    nw_ref, wcv_ref, cw_ref, wout_ref, wi_ref, wo_ref, fnw_ref, o_ref, xe_ref, u_ref = refs
    nw = nw_ref[...]
    cw = cw_ref[...]

    def mixer_pieces(j):
        blk = i * group + j
        lo = j * t
        x = x_ref[lo:lo + t, :]
        mod = mod_refs[j][...]
        v = {}

        def conv_project():
            xe_ref[0:HALO, :] = xp_ref[...] if j == 0 else x_ref[lo - HALO:lo, :]
            xe_ref[HALO:HALO + t, :] = x
            xe_ref[HALO + t:, :] = xn_ref[...] if j == group - 1 else x_ref[lo + t:lo + t + HALO, :]
            h = _rmsnorm(xe_ref[...], nw[1:2]) * (1.0 + mod[4:5]) + mod[3:4]
            v["pj"] = _dot(h.astype(BF16), wcv_ref[...])

        def gated_conv():
            pj = v["pj"]
            u_ref[...] = pj[:, CONV_DIM:2 * CONV_DIM] * pj[:, 2 * CONV_DIM:]
            left_ok = jnp.logical_and(blk != 0, blk != n_lat_blocks)
            right_ok = jnp.logical_and(blk != n_lat_blocks - 1, blk != n_blocks - 1)
            u_ref[0:HALO, :] = jnp.where(left_ok, u_ref[0:HALO, :], 0.0)
            u_ref[HALO + t:, :] = jnp.where(right_ok, u_ref[HALO + t:, :], 0.0)
            conv = (u_ref[HALO - 1:HALO - 1 + t, :] * cw[0:1] + u_ref[HALO:HALO + t, :] * cw[1:2]
                    + u_ref[HALO + 1:HALO + 1 + t, :] * cw[2:3])
            v["conv"] = (pj[HALO:HALO + t, :CONV_DIM] * conv).astype(BF16)

        def mix_out():
            att = att_refs[j][...]
            if has_ctx and j == group - 1:
                att = jnp.where(blk >= n_lat_blocks, attc_ref[...], att)
            mix = _dot(jnp.concatenate([att.T, v["conv"]], axis=1), wout_ref[...])
            x1 = x + mod[5:6] * mix
            v["x1"] = x1
            v["h"] = _ffn_input(x1, nw[2:3], mod[6:7], mod[7:8])

        return v, (conv_project, gated_conv, mix_out)

    v, pieces = mixer_pieces(0)
    for piece in pieces:
        piece()
    for j in range(group):
        nxt_v, sides = mixer_pieces(j + 1) if j + 1 < group else (None, ())
        x2 = _ffn_half_step(v["x1"], v["h"], mod_refs[j][...][8:9], wi_ref, wo_ref, side=sides)
        if final_norm:
            x2 = _rmsnorm(x2, fnw_ref[...])
        o_ref[j * t:(j + 1) * t, :] = x2
        v = nxt_v


def _post(x, att, att_ctx, mods, norm_w, w, final_norm_w, *, layer, n_lat_blocks, n_out_blocks, final_norm):
    batch, tokens, _ = x.shape
    t = TOKEN_BLOCK
    group = GROUP if n_out_blocks % GROUP == 0 else 2
    assert n_out_blocks % group == 0
    tg = t * group
    ng = n_out_blocks // group
    halo_per_step = tg // HALO
    n_halo = tokens // HALO
    has_ctx = att_ctx is not None

    att_args = [att] * group
    att_specs = [pl.BlockSpec((None, HEADS * V_DIM, t),
                              lambda b, i, j=j: (b, 0, jnp.minimum(i * group + j, n_lat_blocks - 1)))
                 for j in range(group)]
    if has_ctx:
        assert att_ctx.shape[2] == t
        att_args.append(att_ctx)
        att_specs.append(pl.BlockSpec((None, HEADS * V_DIM, t), lambda b, i: (b, 0, 0)))
    mod_specs = [pl.BlockSpec((None, N_MOD, D_MODEL),
                              lambda b, i, j=j: (jnp.where(i * group + j >= n_lat_blocks, batch, b), 0, 0))
                 for j in range(group)]
    kernel = functools.partial(_post_kernel, group=group, n_lat_blocks=n_lat_blocks, n_blocks=tokens // t,
                               has_ctx=has_ctx, final_norm=final_norm)
    return pl.pallas_call(
        kernel,
        grid=(batch, ng),
        in_specs=[
            pl.BlockSpec((None, tg, D_MODEL), lambda b, i: (b, i, 0)),
            pl.BlockSpec((None, HALO, D_MODEL), lambda b, i: (b, jnp.maximum(i * halo_per_step - 1, 0), 0)),
            pl.BlockSpec((None, HALO, D_MODEL),
                         lambda b, i: (b, jnp.minimum((i + 1) * halo_per_step, n_halo - 1), 0)),
        ] + att_specs + mod_specs + [
            _const_spec((3, D_MODEL)),
            _const_spec((D_MODEL, 3 * CONV_DIM)),
            _const_spec((3, CONV_DIM)),
            _layer_spec((HEADS * V_DIM + CONV_DIM, D_MODEL), layer),
            _layer_spec((D_MODEL, 2 * D_FF), layer),
            _layer_spec((D_FF, D_MODEL), layer),
            _const_spec((1, D_MODEL)),
        ],
        out_specs=pl.BlockSpec((None, tg, D_MODEL), lambda b, i: (b, i, 0)),
        out_shape=jax.ShapeDtypeStruct((batch, n_out_blocks * t, D_MODEL), F32),
        scratch_shapes=[pltpu.VMEM((t + 2 * HALO, D_MODEL), F32), pltpu.VMEM((t + 2 * HALO, CONV_DIM), F32)],
        compiler_params=_params(),
        name="post",
    )(x, x, x, *att_args, *[mods] * group, norm_w, w["wcv"], w["cw"], w["wout"], w["wi2"], w["wo2"],
      final_norm_w)


def _layer_weights(l, w_in, q_norm_w, kv_norm_w, w_uq, w_ukv, conv_w):
    partner = jnp.array(_ROPE_PARTNER)
    w_in_l = w_in[l]
    kv0 = Q_RANK + KV_RANK
    kpe = w_in_l[:, kv0:kv0 + ROPE]
    wqkv = jnp.concatenate(
        [w_in_l[:, :kv0], kpe, kpe[:, partner], jnp.zeros((D_MODEL, HEAD_PAD - 2 * ROPE), F32)], axis=1)

    uq = w_uq[l].reshape(Q_RANK, HEADS, QK_DIM)
    uq = jnp.concatenate([uq, uq[:, :, NOPE:][:, :, partner]], axis=2)
    wuqt = uq.reshape(Q_RANK, HEADS * HEAD_PAD).T

    ukv = w_ukv[l].reshape(KV_RANK, HEADS, NOPE + V_DIM)
    wk_nope = jnp.concatenate([ukv[:, :, :NOPE], jnp.zeros((KV_RANK, HEADS, HEAD_PAD - NOPE), F32)], axis=2)
    place = np.zeros((KV_RANK, HEADS, HEAD_PAD), np.float32)
    place[np.arange(ROPE), :, NOPE + np.arange(ROPE)] = 1.0
    wk = jnp.concatenate([wk_nope, jnp.asarray(place)], axis=0).reshape(2 * KV_RANK, HEADS * HEAD_PAD)

    wv = jnp.concatenate([ukv[:, :, NOPE:], jnp.zeros((KV_RANK, HEADS, V_ROWS - V_DIM), F32)], axis=2)
    wvt = wv.reshape(KV_RANK, HEADS * V_ROWS).T

    return {
        "wqkv": wqkv.astype(BF16), "wcv": w_in_l[:, kv0 + ROPE:].astype(BF16),
        "qnw": q_norm_w[l].reshape(1, Q_RANK), "kvnw": kv_norm_w[l].reshape(1, KV_RANK),
        "wuqt": wuqt.astype(BF16), "wk": wk.astype(BF16), "wvt": wvt.astype(BF16), "cw": conv_w[l],
    }


def _rope_tables(n_lat, n_ctx):
    t = np.arange(n_lat)
    row = (t // GRID_W).astype(np.float32)
    col = (t % GRID_W).astype(np.float32)
    d_axis = ROPE // 2
    inv = (ROPE_BASE ** (-np.arange(0, d_axis, 2, dtype=np.float32) / d_axis)).astype(np.float32)
    ar = row[:, None] * inv
    ac = col[:, None] * inv
    cr, sr, cc, sc = np.cos(ar), np.sin(ar), np.cos(ac), np.sin(ac)
    cos = np.concatenate([cr, cr, cc, cc], axis=1)
    sin = np.concatenate([-sr, sr, -sc, sc], axis=1)
    cos = np.concatenate([cos, np.ones((n_ctx, ROPE), np.float32)], axis=0)
    sin = np.concatenate([sin, np.zeros((n_ctx, ROPE), np.float32)], axis=0)
    pad = np.zeros((n_lat + n_ctx, HEAD_PAD - ROPE), np.float32)
    tabs = {
        "qcos": (cos * Q_SCALE).T, "qsin": (sin * Q_SCALE).T,
        "kcos": np.concatenate([cos, pad], axis=1), "ksin": np.concatenate([sin, pad], axis=1),
    }
    return {name: jnp.asarray(np.ascontiguousarray(v, dtype=np.float32)) for name, v in tabs.items()}


def kernel(x, c, ctx, c_ctx, w_ada, b_ada, norm_w, w_ffn1_in, w_ffn1_out, w_ffn2_in, w_ffn2_out, w_in, q_norm_w,
           kv_norm_w, w_uq, w_ukv, conv_w, w_out, final_norm_w):
    batch, n_lat, _ = x.shape
    n_ctx = ctx.shape[1]
    depth = w_ada.shape[0]
    assert n_ctx % TOKEN_BLOCK == 0 and n_lat % QUERY_BLOCK == 0 and QUERY_BLOCK % TOKEN_BLOCK == 0
    assert batch + 1 <= 8
    n_lat_blocks = n_lat // TOKEN_BLOCK
    n_blocks = (n_lat + n_ctx) // TOKEN_BLOCK

    cond = jnp.concatenate([c, c_ctx[None, :], jnp.zeros((8 - batch - 1, D_MODEL), F32)], axis=0)
    mods = _ada(cond, w_ada, b_ada).reshape(depth, 8, N_MOD, D_MODEL)
    tabs = _rope_tables(n_lat, n_ctx)
    fnw = final_norm_w.reshape(1, D_MODEL)
    big = {"wi1": w_ffn1_in.astype(BF16), "wo1": w_ffn1_out.astype(BF16), "wi2": w_ffn2_in.astype(BF16),
           "wo2": w_ffn2_out.astype(BF16), "wout": w_out.astype(BF16)}

    xs, xs_ctx = x, ctx
    for l in range(depth):
        last = l == depth - 1
        w = dict(big, **_layer_weights(l, w_in, q_norm_w, kv_norm_w, w_uq, w_ukv, conv_w))
        xs, qt, k, vt = _pre(xs, xs_ctx, mods[l], norm_w[l], w, tabs, l, n_lat_blocks)
        xs_ctx = None
        att = _attn(qt, k, vt, n_q_blocks=n_lat // QUERY_BLOCK)
        att_ctx = None if last else _attn_ctx(qt, k, vt, n_ctx=n_ctx)
        xs = _post(xs, att, att_ctx, mods[l], norm_w[l], w, fnw, layer=l, n_lat_blocks=n_lat_blocks,
                   n_out_blocks=n_lat_blocks if last else n_blocks, final_norm=last)
    return xs
```

```python
import functools
import math

import jax
import jax.numpy as jnp
import numpy as np
from jax import lax
from jax.experimental import pallas as pl
from jax.experimental.pallas import tpu as pltpu

F32 = jnp.float32
BF16 = jnp.bfloat16

D_MODEL = 1024
HEADS = 8
NOPE = 64
ROPE = 32
QK_DIM = NOPE + ROPE
V_DIM = 64
Q_RANK = 256
KV_RANK = 128
CONV_DIM = 512
D_FF = 2816
N_MOD = 9
GRID_W = 64
ROPE_BASE = 10000.0
EPS = 1e-6

HEAD_PAD = 128
V_ROWS = 80
TOKEN_BLOCK = 256
GROUP = 3
LATENT_GROUP = 4
QUERY_BLOCK = 512
KEY_CHUNK = 256
MAX_ROWS = 16
FF_CHUNK = 256
FF_AHEAD = 2
HALO = 8
ATTN_HEADS_PER_STEP = 2
VMEM_LIMIT_BYTES = 56 * 1024 * 1024

Q_SCALE = math.log2(math.e) / math.sqrt(QK_DIM)

_ROPE_PARTNER = tuple(range(8, 16)) + tuple(range(0, 8)) + tuple(range(24, 32)) + tuple(range(16, 24))


def _dot(a, b):
    return jnp.dot(a, b, preferred_element_type=F32)


def _dot_nt(a, b):
    return lax.dot_general(a, b, (((1,), (1,)), ((), ())), preferred_element_type=F32)


def _rmsnorm(x, w):
    r = lax.rsqrt(jnp.mean(x * x, axis=-1, keepdims=True) + EPS)
    return (x * r) * w


def _const_spec(shape):
    return pl.BlockSpec(shape, lambda *_: (0,) * len(shape), pipeline_mode=pl.Buffered(1))


def _layer_spec(shape, layer):
    return pl.BlockSpec((None,) + shape, lambda *_: (layer,) + (0,) * len(shape), pipeline_mode=pl.Buffered(1))


def _params():
    return pltpu.CompilerParams(
        dimension_semantics=("arbitrary", "arbitrary"), vmem_limit_bytes=VMEM_LIMIT_BYTES)


def _ada_kernel(c_ref, w_ref, b_ref, o_ref):
    c = c_ref[...]
    act = (c * (1.0 / (1.0 + jnp.exp(-c)))).astype(BF16)
    o_ref[0] = _dot(act, w_ref[0].astype(BF16)) + b_ref[0]


def _ada(cond, w_ada, b_ada):
    depth = w_ada.shape[0]
    n_out = w_ada.shape[2]
    blk = 2304
    assert n_out % blk == 0
    return pl.pallas_call(
        _ada_kernel,
        grid=(depth, n_out // blk),
        in_specs=[
            pl.BlockSpec((8, D_MODEL), lambda l, j: (0, 0)),
            pl.BlockSpec((1, D_MODEL, blk), lambda l, j: (l, 0, j)),
            pl.BlockSpec((1, 1, blk), lambda l, j: (l, 0, j)),
        ],
        out_specs=pl.BlockSpec((1, 8, blk), lambda l, j: (l, 0, j)),
        out_shape=jax.ShapeDtypeStruct((depth, 8, n_out), F32),
        compiler_params=_params(),
        name="ada",
    )(cond, w_ada, b_ada.reshape(depth, 1, n_out))


def _ffn_input(x, norm_w, shift, scale):
    return (_rmsnorm(x, norm_w) * (1.0 + scale) + shift).astype(BF16)


def _ffn_half_step(x, h, gate, wi_ref, wo_ref, side=()):
    side = list(side)
    n_chunks = D_FF // FF_CHUNK

    def gate_up(c):
        lo = c * FF_CHUNK
        return _dot(h, wi_ref[:, lo:lo + FF_CHUNK]), _dot(h, wi_ref[:, D_FF + lo:D_FF + lo + FF_CHUNK])

    pending = [gate_up(c) for c in range(FF_AHEAD)]
    acc = None
    for c in range(n_chunks):
        if c + FF_AHEAD < n_chunks:
            pending.append(gate_up(c + FF_AHEAD))
        g, u = pending.pop(0)
        a = (g * (1.0 / (1.0 + jnp.exp(-g))) * u).astype(BF16)
        y = _dot(a, wo_ref[c * FF_CHUNK:(c + 1) * FF_CHUNK, :])
        acc = y if acc is None else acc + y
        if side and c % 2 == 0:
            side.pop(0)()
    for piece in side:
        piece()
    return x + (0.5 * gate) * acc


def _mla_pieces(x1, mod, nw, rows, refs):
    (wqkv_ref, qnw_ref, kvnw_ref, wuqt_ref, wk_ref, wvt_ref, qcos_ref, qsin_ref, kcos_ref, ksin_ref,
     qt_ref, k_ref, vt_ref) = refs
    v = {}

    def project():
        h = _rmsnorm(x1, nw[1:2]) * (1.0 + mod[4:5]) + mod[3:4]
        v["pj"] = _dot(h.astype(BF16), wqkv_ref[...])

    def low_rank_norms():
        pj = v["pj"]
        v["nq"] = _rmsnorm(pj[:, :Q_RANK], qnw_ref[...]).astype(BF16)
        v["nkv"] = _rmsnorm(pj[:, Q_RANK:Q_RANK + KV_RANK], kvnw_ref[...]).astype(BF16)
        v["kp"] = pj[:, Q_RANK + KV_RANK:]

    def queries():
        qt = _dot_nt(wuqt_ref[...], v["nq"])
        qcos = qcos_ref[:, rows]
        qsin = qsin_ref[:, rows]
        for hd in range(HEADS):
            r = hd * HEAD_PAD
            qt_ref[r:r + NOPE, rows] = (qt[r:r + NOPE] * Q_SCALE).astype(BF16)
            rot = qt[r + NOPE:r + QK_DIM] * qcos + qt[r + QK_DIM:r + HEAD_PAD] * qsin
            qt_ref[r + NOPE:r + QK_DIM, rows] = rot.astype(BF16)
            qt_ref[r + QK_DIM:r + HEAD_PAD, rows] = jnp.zeros((HEAD_PAD - QK_DIM, qt.shape[1]), BF16)

    def keys():
        kp = v["kp"]
        krot = kp * kcos_ref[rows, :] + pltpu.roll(kp, HEAD_PAD - ROPE, axis=1) * ksin_ref[rows, :]
        k_ref[rows, :] = _dot(jnp.concatenate([v["nkv"], krot.astype(BF16)], axis=1), wk_ref[...]).astype(BF16)

    def values():
        vt = _dot_nt(wvt_ref[...], v["nkv"])
        row = lax.broadcasted_iota(jnp.int32, vt.shape, 0)
        vt_ref[:, rows] = jnp.where(row % V_ROWS == V_DIM, 1.0, vt).astype(BF16)

    return (project, low_rank_norms, queries, keys, values)


def _pre_kernel(*refs, n_groups, split_input):
    t = TOKEN_BLOCK
    if split_input:
        *x_refs, xc_ref = refs[:GROUP + 1]
        refs = refs[GROUP + 1:]
        xs = [r[...] for r in x_refs]
        xs[-1] = jnp.where(pl.program_id(1) == n_groups - 1, xc_ref[...], xs[-1])
    else:
        x_ref, *refs = refs
        xs = [x_ref[j * t:(j + 1) * t, :] for j in range(GROUP)]
    mod_refs, refs = refs[:GROUP], refs[GROUP:]
    nw_ref, wi_ref, wo_ref, *side_refs, xo_ref, qt_ref, k_ref, vt_ref = refs
    side_refs = (*side_refs, qt_ref, k_ref, vt_ref)
    nw = nw_ref[...]

    sides = ()
    for j in range(GROUP):
        rows = slice(j * t, (j + 1) * t)
        mod = mod_refs[j][...]
        x1 = _ffn_half_step(xs[j], _ffn_input(xs[j], nw[0:1], mod[0:1], mod[1:2]), mod[2:3], wi_ref, wo_ref,
                            side=sides)
        xo_ref[rows, :] = x1
        sides = _mla_pieces(x1, mod, nw, rows, side_refs)
    for piece in sides:
        piece()


def _pre(x, x_ctx, mods, norm_w, w, tabs, layer, n_lat_blocks):
    batch = x.shape[0]
    t = TOKEN_BLOCK
    tg = t * GROUP
    split_input = x_ctx is not None
    tokens = x.shape[1] + (x_ctx.shape[1] if split_input else 0)
    assert tokens % tg == 0
    ng = tokens // tg

    if split_input:
        assert x_ctx.shape[1] == t
        x_args = [x] * GROUP + [x_ctx]
        x_specs = [pl.BlockSpec((None, t, D_MODEL),
                                lambda b, i, j=j: (b, jnp.minimum(i * GROUP + j, n_lat_blocks - 1), 0))
                   for j in range(GROUP)]
        x_specs.append(pl.BlockSpec((None, t, D_MODEL), lambda b, i: (b, 0, 0)))
    else:
        x_args = [x]
        x_specs = [pl.BlockSpec((None, tg, D_MODEL), lambda b, i: (b, i, 0))]
    mod_specs = [pl.BlockSpec((None, N_MOD, D_MODEL),
                              lambda b, i, j=j: (jnp.where(i * GROUP + j >= n_lat_blocks, batch, b), 0, 0))
                 for j in range(GROUP)]

    return pl.pallas_call(
        functools.partial(_pre_kernel, n_groups=ng, split_input=split_input),
        grid=(batch, ng),
        in_specs=x_specs + mod_specs + [
            _const_spec((3, D_MODEL)),
            _layer_spec((D_MODEL, 2 * D_FF), layer),
            _layer_spec((D_FF, D_MODEL), layer),
            _layer_spec((D_MODEL, 512), layer),
            _layer_spec((1, Q_RANK), layer),
            _layer_spec((1, KV_RANK), layer),
            _layer_spec((HEADS * HEAD_PAD, Q_RANK), layer),
            _layer_spec((2 * KV_RANK, HEADS * HEAD_PAD), layer),
            _layer_spec((HEADS * V_ROWS, KV_RANK), layer),
            pl.BlockSpec((ROPE, tg), lambda b, i: (0, i)),
            pl.BlockSpec((ROPE, tg), lambda b, i: (0, i)),
            pl.BlockSpec((tg, HEAD_PAD), lambda b, i: (i, 0)),
            pl.BlockSpec((tg, HEAD_PAD), lambda b, i: (i, 0)),
        ],
        out_specs=[
            pl.BlockSpec((None, tg, D_MODEL), lambda b, i: (b, i, 0)),
            pl.BlockSpec((None, HEADS * HEAD_PAD, tg), lambda b, i: (b, 0, i)),
            pl.BlockSpec((None, tg, HEADS * HEAD_PAD), lambda b, i: (b, i, 0)),
            pl.BlockSpec((None, HEADS * V_ROWS, tg), lambda b, i: (b, 0, i)),
        ],
        out_shape=[
            jax.ShapeDtypeStruct((batch, tokens, D_MODEL), F32),
            jax.ShapeDtypeStruct((batch, HEADS * HEAD_PAD, tokens), BF16),
            jax.ShapeDtypeStruct((batch, tokens, HEADS * HEAD_PAD), BF16),
            jax.ShapeDtypeStruct((batch, HEADS * V_ROWS, tokens), BF16),
        ],
        compiler_params=_params(),
        name="pre",
    )(*x_args, *[mods] * GROUP, norm_w, w["wi1"], w["wo1"], w["wqkv"], w["qnw"], w["kvnw"], w["wuqt"], w["wk"],
      w["wvt"], tabs["qcos"], tabs["qsin"], tabs["kcos"], tabs["ksin"])


def _column_max(s, mx):
    parts = [s[r:r + MAX_ROWS] for r in range(0, s.shape[0], MAX_ROWS)]
    if mx is not None:
        parts.append(mx)
    while len(parts) > 1:
        nxt = [jnp.maximum(parts[j], parts[j + 1]) for j in range(0, len(parts) - 1, 2)]
        if len(parts) % 2:
            nxt.append(parts[-1])
        parts = nxt
    return parts[0]


def _scores(k_ref, qt, s_ref, m_ref, n_chunks, key_chunk):
    mx = None
    for c in range(n_chunks):
        rows = slice(c * key_chunk, (c + 1) * key_chunk)
        s = _dot(k_ref[rows, :], qt)
        s_ref[rows, :] = s
        mx = _column_max(s, mx)
    m_ref[...] = jnp.max(mx, axis=0, keepdims=True)


def _attn_kernel(qt0_ref, k0_ref, qtn_ref, kn_ref, vt_ref, o_ref, s_ref, m_ref, *, n_keys, key_chunk):
    n_chunks = n_keys // key_chunk
    tq = qtn_ref.shape[1]
    first = jnp.logical_and(jnp.logical_and(pl.program_id(0) == 0, pl.program_id(1) == 0),
                            pl.program_id(2) == 0)

    @pl.when(first)
    def _():
        _scores(k0_ref, qt0_ref[...], s_ref, m_ref, n_chunks, key_chunk)

    m = m_ref[...]
    qtn = qtn_ref[...]
    acc = jnp.zeros((V_ROWS, tq), F32)
    mx = None
    for c in range(n_chunks):
        rows = slice(c * key_chunk, (c + 1) * key_chunk)
        p = jnp.exp2(s_ref[rows, :] - m).astype(BF16)
        s = _dot(kn_ref[rows, :], qtn)
        acc = acc + _dot(vt_ref[:, rows], p)
        s_ref[rows, :] = s
        mx = _column_max(s, mx)
    m_ref[...] = jnp.max(mx, axis=0, keepdims=True)
    o_ref[...] = (acc[:V_DIM] * (1.0 / acc[V_DIM:V_DIM + 1])).astype(BF16)


def _attn(qt, k, vt, *, n_q_blocks):
    batch, _, tokens = qt.shape
    tq = QUERY_BLOCK
    assert tokens % KEY_CHUNK == 0

    def nxt(b, h, i):
        wrap_i = i == n_q_blocks - 1
        wrap_h = jnp.logical_and(wrap_i, h == HEADS - 1)
        last = jnp.logical_and(wrap_h, b == batch - 1)
        i2 = jnp.where(wrap_i, 0, i + 1)
        h2 = jnp.where(wrap_i, jnp.where(h == HEADS - 1, 0, h + 1), h)
        b2 = jnp.where(wrap_h, b + 1, b)
        return jnp.where(last, b, b2), jnp.where(last, h, h2), jnp.where(last, i, i2)

    def qt_next(b, h, i):
        b2, h2, i2 = nxt(b, h, i)
        return (b2, h2, i2)

    def k_next(b, h, i):
        b2, h2, _ = nxt(b, h, i)
        return (b2, 0, h2)

    kernel = functools.partial(_attn_kernel, n_keys=tokens, key_chunk=KEY_CHUNK)
    return pl.pallas_call(
        kernel,
        grid=(batch, HEADS, n_q_blocks),
        in_specs=[
            pl.BlockSpec((None, HEAD_PAD, tq), lambda b, h, i: (0, 0, 0), pipeline_mode=pl.Buffered(1)),
            pl.BlockSpec((None, tokens, HEAD_PAD), lambda b, h, i: (0, 0, 0), pipeline_mode=pl.Buffered(1)),
            pl.BlockSpec((None, HEAD_PAD, tq), qt_next),
            pl.BlockSpec((None, tokens, HEAD_PAD), k_next),
            pl.BlockSpec((None, V_ROWS, tokens), lambda b, h, i: (b, h, 0)),
        ],
        out_specs=pl.BlockSpec((None, V_DIM, tq), lambda b, h, i: (b, h, i)),
        out_shape=jax.ShapeDtypeStruct((batch, HEADS * V_DIM, n_q_blocks * tq), BF16),
        scratch_shapes=[pltpu.VMEM((tokens, tq), F32), pltpu.VMEM((1, tq), F32)],
        compiler_params=pltpu.CompilerParams(
            dimension_semantics=("arbitrary", "arbitrary", "arbitrary"), vmem_limit_bytes=VMEM_LIMIT_BYTES),
        name="attn",
    )(qt, k, qt, k, vt)


def _attn_ctx_kernel(qt_ref, k_ref, vt_ref, o_ref):
    for hd in range(ATTN_HEADS_PER_STEP):
        s = _dot(k_ref[:, hd * HEAD_PAD:(hd + 1) * HEAD_PAD], qt_ref[hd * HEAD_PAD:(hd + 1) * HEAD_PAD, :])
        p = jnp.exp2(s - jnp.max(s, axis=0, keepdims=True)).astype(BF16)
        acc = _dot(vt_ref[hd * V_ROWS:(hd + 1) * V_ROWS, :], p)
        o_ref[hd * V_DIM:(hd + 1) * V_DIM, :] = (acc[:V_DIM] * (1.0 / acc[V_DIM:V_DIM + 1])).astype(BF16)


def _attn_ctx(qt, k, vt, *, n_ctx):
    batch, _, tokens = qt.shape
    hb = ATTN_HEADS_PER_STEP
    assert tokens % n_ctx == 0
    c = tokens // n_ctx - 1
    return pl.pallas_call(
        _attn_ctx_kernel,
        grid=(batch, HEADS // hb),
        in_specs=[
            pl.BlockSpec((None, hb * HEAD_PAD, n_ctx), lambda b, g: (b, g, c)),
            pl.BlockSpec((None, n_ctx, hb * HEAD_PAD), lambda b, g: (b, c, g)),
            pl.BlockSpec((None, hb * V_ROWS, n_ctx), lambda b, g: (b, g, c)),
        ],
        out_specs=pl.BlockSpec((None, hb * V_DIM, n_ctx), lambda b, g: (b, g, 0)),
        out_shape=jax.ShapeDtypeStruct((batch, HEADS * V_DIM, n_ctx), BF16),
        compiler_params=_params(),
        name="attn_ctx",
    )(qt, k, vt)


def _post_kernel(*refs, group, n_lat_blocks, n_blocks, has_ctx, final_norm):
    t = TOKEN_BLOCK
    i = pl.program_id(1)
    x_ref, xp_ref, xn_ref, *refs = refs
    att_refs, refs = refs[:group], refs[group:]
    if has_ctx:
        attc_ref, *refs = refs
    mod_refs, refs = refs[:group], refs[group:]
    nw_ref, wcv_ref, cw_ref, wout_ref, wi_ref, wo_ref, fnw_ref, o_ref, xe_ref, u_ref = refs
    nw = nw_ref[...]
    cw = cw_ref[...]

    def mixer_pieces(j):
        blk = i * group + j
        lo = j * t
        x = x_ref[lo:lo + t, :]
        mod = mod_refs[j][...]
        v = {}

        def conv_project():
            xe_ref[0:HALO, :] = xp_ref[...] if j == 0 else x_ref[lo - HALO:lo, :]
            xe_ref[HALO:HALO + t, :] = x
            xe_ref[HALO + t:, :] = xn_ref[...] if j == group - 1 else x_ref[lo + t:lo + t + HALO, :]
            h = _rmsnorm(xe_ref[...], nw[1:2]) * (1.0 + mod[4:5]) + mod[3:4]
            v["pj"] = _dot(h.astype(BF16), wcv_ref[...])

        def gated_conv():
            pj = v["pj"]
            u_ref[...] = pj[:, CONV_DIM:2 * CONV_DIM] * pj[:, 2 * CONV_DIM:]
            left_ok = jnp.logical_and(blk != 0, blk != n_lat_blocks)
            right_ok = jnp.logical_and(blk != n_lat_blocks - 1, blk != n_blocks - 1)
            u_ref[0:HALO, :] = jnp.where(left_ok, u_ref[0:HALO, :], 0.0)
            u_ref[HALO + t:, :] = jnp.where(right_ok, u_ref[HALO + t:, :], 0.0)
            conv = (u_ref[HALO - 1:HALO - 1 + t, :] * cw[0:1] + u_ref[HALO:HALO + t, :] * cw[1:2]
                    + u_ref[HALO + 1:HALO + 1 + t, :] * cw[2:3])
            v["conv"] = (pj[HALO:HALO + t, :CONV_DIM] * conv).astype(BF16)

        def mix_out():
            att = att_refs[j][...]
            if has_ctx and j == group - 1:
                att = jnp.where(blk >= n_lat_blocks, attc_ref[...], att)
            mix = _dot(jnp.concatenate([att.T, v["conv"]], axis=1), wout_ref[...])
            x1 = x + mod[5:6] * mix
            v["x1"] = x1
            v["h"] = _ffn_input(x1, nw[2:3], mod[6:7], mod[7:8])

        return v, (conv_project, gated_conv, mix_out)

    v, pieces = mixer_pieces(0)
    for piece in pieces:
        piece()
    for j in range(group):
        nxt_v, sides = mixer_pieces(j + 1) if j + 1 < group else (None, ())
        x2 = _ffn_half_step(v["x1"], v["h"], mod_refs[j][...][8:9], wi_ref, wo_ref, side=sides)
        if final_norm:
            x2 = _rmsnorm(x2, fnw_ref[...])
        o_ref[j * t:(j + 1) * t, :] = x2
        v = nxt_v


def _post(x, att, att_ctx, mods, norm_w, w, final_norm_w, *, layer, n_lat_blocks, n_out_blocks, final_norm):
    batch, tokens, _ = x.shape
    t = TOKEN_BLOCK
    group = GROUP if n_out_blocks % GROUP == 0 else LATENT_GROUP
    assert n_out_blocks % group == 0
    tg = t * group
    ng = n_out_blocks // group
    halo_per_step = tg // HALO
    n_halo = tokens // HALO
    has_ctx = att_ctx is not None

    att_args = [att] * group
    att_specs = [pl.BlockSpec((None, HEADS * V_DIM, t),
                              lambda b, i, j=j: (b, 0, jnp.minimum(i * group + j, n_lat_blocks - 1)))
                 for j in range(group)]
    if has_ctx:
        assert att_ctx.shape[2] == t
        att_args.append(att_ctx)
        att_specs.append(pl.BlockSpec((None, HEADS * V_DIM, t), lambda b, i: (b, 0, 0)))
    mod_specs = [pl.BlockSpec((None, N_MOD, D_MODEL),
                              lambda b, i, j=j: (jnp.where(i * group + j >= n_lat_blocks, batch, b), 0, 0))
                 for j in range(group)]
    kernel = functools.partial(_post_kernel, group=group, n_lat_blocks=n_lat_blocks, n_blocks=tokens // t,
                               has_ctx=has_ctx, final_norm=final_norm)
    return pl.pallas_call(
        kernel,
        grid=(batch, ng),
        in_specs=[
            pl.BlockSpec((None, tg, D_MODEL), lambda b, i: (b, i, 0)),
            pl.BlockSpec((None, HALO, D_MODEL), lambda b, i: (b, jnp.maximum(i * halo_per_step - 1, 0), 0)),
            pl.BlockSpec((None, HALO, D_MODEL),
                         lambda b, i: (b, jnp.minimum((i + 1) * halo_per_step, n_halo - 1), 0)),
        ] + att_specs + mod_specs + [
            _const_spec((3, D_MODEL)),
            _layer_spec((D_MODEL, 3 * CONV_DIM), layer),
            _layer_spec((3, CONV_DIM), layer),
            _layer_spec((HEADS * V_DIM + CONV_DIM, D_MODEL), layer),
            _layer_spec((D_MODEL, 2 * D_FF), layer),
            _layer_spec((D_FF, D_MODEL), layer),
            _const_spec((1, D_MODEL)),
        ],
        out_specs=pl.BlockSpec((None, tg, D_MODEL), lambda b, i: (b, i, 0)),
        out_shape=jax.ShapeDtypeStruct((batch, n_out_blocks * t, D_MODEL), F32),
        scratch_shapes=[pltpu.VMEM((t + 2 * HALO, D_MODEL), F32), pltpu.VMEM((t + 2 * HALO, CONV_DIM), F32)],
        compiler_params=_params(),
        name="post",
    )(x, x, x, *att_args, *[mods] * group, norm_w, w["wcv"], w["cw"], w["wout"], w["wi2"], w["wo2"],
      final_norm_w)


def _mixer_weights(w_in, q_norm_w, kv_norm_w, w_uq, w_ukv, conv_w):
    depth = w_in.shape[0]
    partner = jnp.array(_ROPE_PARTNER)
    kv0 = Q_RANK + KV_RANK
    kpe = w_in[:, :, kv0:kv0 + ROPE]
    wqkv = jnp.concatenate(
        [w_in[:, :, :kv0], kpe, kpe[:, :, partner], jnp.zeros((depth, D_MODEL, HEAD_PAD - 2 * ROPE), F32)], axis=2)

    uq = w_uq.reshape(depth, Q_RANK, HEADS, QK_DIM)
    uq = jnp.concatenate([uq, uq[..., NOPE:][..., partner]], axis=3)
    wuqt = jnp.swapaxes(uq.reshape(depth, Q_RANK, HEADS * HEAD_PAD), 1, 2)

    ukv = w_ukv.reshape(depth, KV_RANK, HEADS, NOPE + V_DIM)
    wk_nope = jnp.concatenate(
        [ukv[..., :NOPE], jnp.zeros((depth, KV_RANK, HEADS, HEAD_PAD - NOPE), F32)], axis=3)
    place = np.zeros((depth, KV_RANK, HEADS, HEAD_PAD), np.float32)
    place[:, np.arange(ROPE), :, NOPE + np.arange(ROPE)] = 1.0
    wk = jnp.concatenate([wk_nope, jnp.asarray(place)], axis=1).reshape(depth, 2 * KV_RANK, HEADS * HEAD_PAD)

    wv = jnp.concatenate([ukv[..., NOPE:], jnp.zeros((depth, KV_RANK, HEADS, V_ROWS - V_DIM), F32)], axis=3)
    wvt = jnp.swapaxes(wv.reshape(depth, KV_RANK, HEADS * V_ROWS), 1, 2)

    return {
        "wqkv": wqkv.astype(BF16), "wcv": w_in[:, :, kv0 + ROPE:].astype(BF16),
        "qnw": q_norm_w.reshape(depth, 1, Q_RANK), "kvnw": kv_norm_w.reshape(depth, 1, KV_RANK),
        "wuqt": wuqt.astype(BF16), "wk": wk.astype(BF16), "wvt": wvt.astype(BF16), "cw": conv_w,
    }


def _rope_tables(n_lat, n_ctx):
    t = np.arange(n_lat)
    row = (t // GRID_W).astype(np.float32)
    col = (t % GRID_W).astype(np.float32)
    d_axis = ROPE // 2
    inv = (ROPE_BASE ** (-np.arange(0, d_axis, 2, dtype=np.float32) / d_axis)).astype(np.float32)
    ar = row[:, None] * inv
    ac = col[:, None] * inv
    cr, sr, cc, sc = np.cos(ar), np.sin(ar), np.cos(ac), np.sin(ac)
    cos = np.concatenate([cr, cr, cc, cc], axis=1)
    sin = np.concatenate([-sr, sr, -sc, sc], axis=1)
    cos = np.concatenate([cos, np.ones((n_ctx, ROPE), np.float32)], axis=0)
    sin = np.concatenate([sin, np.zeros((n_ctx, ROPE), np.float32)], axis=0)
    pad = np.zeros((n_lat + n_ctx, HEAD_PAD - ROPE), np.float32)
    tabs = {
        "qcos": (cos * Q_SCALE).T, "qsin": (sin * Q_SCALE).T,
        "kcos": np.concatenate([cos, pad], axis=1), "ksin": np.concatenate([sin, pad], axis=1),
    }
    return {name: jnp.asarray(np.ascontiguousarray(v, dtype=np.float32)) for name, v in tabs.items()}


def kernel(x, c, ctx, c_ctx, w_ada, b_ada, norm_w, w_ffn1_in, w_ffn1_out, w_ffn2_in, w_ffn2_out, w_in, q_norm_w,
           kv_norm_w, w_uq, w_ukv, conv_w, w_out, final_norm_w):
    batch, n_lat, _ = x.shape
    n_ctx = ctx.shape[1]
    depth = w_ada.shape[0]
    assert n_ctx % TOKEN_BLOCK == 0 and n_lat % QUERY_BLOCK == 0 and QUERY_BLOCK % TOKEN_BLOCK == 0
    assert batch + 1 <= 8
    n_lat_blocks = n_lat // TOKEN_BLOCK
    n_blocks = (n_lat + n_ctx) // TOKEN_BLOCK

    cond = jnp.concatenate([c, c_ctx[None, :], jnp.zeros((8 - batch - 1, D_MODEL), F32)], axis=0)
    mods = _ada(cond, w_ada, b_ada).reshape(depth, 8, N_MOD, D_MODEL)
    tabs = _rope_tables(n_lat, n_ctx)
    fnw = final_norm_w.reshape(1, D_MODEL)
    w = {"wi1": w_ffn1_in.astype(BF16), "wo1": w_ffn1_out.astype(BF16), "wi2": w_ffn2_in.astype(BF16),
         "wo2": w_ffn2_out.astype(BF16), "wout": w_out.astype(BF16)}
    w.update(_mixer_weights(w_in, q_norm_w, kv_norm_w, w_uq, w_ukv, conv_w))

    xs, xs_ctx = x, ctx
    for l in range(depth):
        last = l == depth - 1
        xs, qt, k, vt = _pre(xs, xs_ctx, mods[l], norm_w[l], w, tabs, l, n_lat_blocks)
        xs_ctx = None
        att = _attn(qt, k, vt, n_q_blocks=n_lat // QUERY_BLOCK)
        att_ctx = None if last else _attn_ctx(qt, k, vt, n_ctx=n_ctx)
        xs = _post(xs, att, att_ctx, mods[l], norm_w[l], w, fnw, layer=l, n_lat_blocks=n_lat_blocks,
                   n_out_blocks=n_lat_blocks if last else n_blocks, final_norm=last)
    return xs
```

```python
import functools
import math

import jax
import jax.numpy as jnp
import numpy as np
from jax import lax
from jax.experimental import pallas as pl
from jax.experimental.pallas import tpu as pltpu

F32 = jnp.float32
BF16 = jnp.bfloat16

D_MODEL = 1024
HEADS = 8
NOPE = 64
ROPE = 32
QK_DIM = NOPE + ROPE
V_DIM = 64
Q_RANK = 256
KV_RANK = 128
CONV_DIM = 512
D_FF = 2816
N_MOD = 9
GRID_W = 64
ROPE_BASE = 10000.0
EPS = 1e-6

HEAD_PAD = 128
V_ROWS = 80
TOKEN_BLOCK = 256
GROUP = 3
LATENT_GROUP = 4
QUERY_BLOCK = 512
KEY_CHUNK = 256
MAX_ROWS = 16
FF_CHUNK = 256
FF_AHEAD = 2
HALO = 8
ATTN_HEADS_PER_STEP = 2
VMEM_LIMIT_BYTES = 56 * 1024 * 1024

Q_SCALE = math.log2(math.e) / math.sqrt(QK_DIM)

_ROPE_PARTNER = tuple(range(8, 16)) + tuple(range(0, 8)) + tuple(range(24, 32)) + tuple(range(16, 24))


def _dot(a, b):
    return jnp.dot(a, b, preferred_element_type=F32)


def _dot_nt(a, b):
    return lax.dot_general(a, b, (((1,), (1,)), ((), ())), preferred_element_type=F32)


def _rmsnorm(x, w):
    r = lax.rsqrt(jnp.mean(x * x, axis=-1, keepdims=True) + EPS)
    return (x * r) * w


def _const_spec(shape):
    return pl.BlockSpec(shape, lambda *_: (0,) * len(shape), pipeline_mode=pl.Buffered(1))


def _layer_spec(shape, layer):
    return pl.BlockSpec((None,) + shape, lambda *_: (layer,) + (0,) * len(shape), pipeline_mode=pl.Buffered(1))


def _params():
    return pltpu.CompilerParams(
        dimension_semantics=("arbitrary", "arbitrary"), vmem_limit_bytes=VMEM_LIMIT_BYTES)


def _ada_kernel(c_ref, w_ref, b_ref, o_ref):
    c = c_ref[...]
    act = (c * (1.0 / (1.0 + jnp.exp(-c)))).astype(BF16)
    o_ref[0] = _dot(act, w_ref[0].astype(BF16)) + b_ref[0]


def _ada(cond, w_ada, b_ada):
    depth = w_ada.shape[0]
    n_out = w_ada.shape[2]
    blk = 2304
    assert n_out % blk == 0
    return pl.pallas_call(
        _ada_kernel,
        grid=(depth, n_out // blk),
        in_specs=[
            pl.BlockSpec((8, D_MODEL), lambda l, j: (0, 0)),
            pl.BlockSpec((1, D_MODEL, blk), lambda l, j: (l, 0, j)),
            pl.BlockSpec((1, 1, blk), lambda l, j: (l, 0, j)),
        ],
        out_specs=pl.BlockSpec((1, 8, blk), lambda l, j: (l, 0, j)),
        out_shape=jax.ShapeDtypeStruct((depth, 8, n_out), F32),
        compiler_params=_params(),
        name="ada",
    )(cond, w_ada, b_ada.reshape(depth, 1, n_out))


def _ffn_input(x, norm_w, shift, scale):
    return (_rmsnorm(x, norm_w) * (1.0 + scale) + shift).astype(BF16)


def _ffn_half_step(x, h, gate, wi_ref, wo_ref, side=()):
    side = list(side)
    n_chunks = D_FF // FF_CHUNK

    def gate_up(c):
        lo = c * FF_CHUNK
        return _dot(h, wi_ref[:, lo:lo + FF_CHUNK]), _dot(h, wi_ref[:, D_FF + lo:D_FF + lo + FF_CHUNK])

    pending = [gate_up(c) for c in range(FF_AHEAD)]
    acc = None
    for c in range(n_chunks):
        if c + FF_AHEAD < n_chunks:
            pending.append(gate_up(c + FF_AHEAD))
        g, u = pending.pop(0)
        a = (g * (1.0 / (1.0 + jnp.exp(-g))) * u).astype(BF16)
        y = _dot(a, wo_ref[c * FF_CHUNK:(c + 1) * FF_CHUNK, :])
        acc = y if acc is None else acc + y
        if side and c % 2 == 1:
            side.pop(0)()
    for piece in side:
        piece()
    return x + (0.5 * gate) * acc


def _mla_pieces(x1, mod, nw, rows, refs):
    (wqkv_ref, qnw_ref, kvnw_ref, wuqt_ref, wk_ref, wvt_ref, qcos_ref, qsin_ref, kcos_ref, ksin_ref,
     qt_ref, k_ref, vt_ref) = refs
    v = {}

    def project():
        h = _rmsnorm(x1, nw[1:2]) * (1.0 + mod[4:5]) + mod[3:4]
        v["pj"] = _dot(h.astype(BF16), wqkv_ref[...])

    def low_rank_norms():
        pj = v["pj"]
        v["nq"] = _rmsnorm(pj[:, :Q_RANK], qnw_ref[...]).astype(BF16)
        v["nkv"] = _rmsnorm(pj[:, Q_RANK:Q_RANK + KV_RANK], kvnw_ref[...]).astype(BF16)
        v["kp"] = pj[:, Q_RANK + KV_RANK:]

    def queries():
        qt = _dot_nt(wuqt_ref[...], v["nq"])
        qcos = qcos_ref[:, rows]
        qsin = qsin_ref[:, rows]
        for hd in range(HEADS):
            r = hd * HEAD_PAD
            qt_ref[r:r + NOPE, rows] = (qt[r:r + NOPE] * Q_SCALE).astype(BF16)
            rot = qt[r + NOPE:r + QK_DIM] * qcos + qt[r + QK_DIM:r + HEAD_PAD] * qsin
            qt_ref[r + NOPE:r + QK_DIM, rows] = rot.astype(BF16)
            qt_ref[r + QK_DIM:r + HEAD_PAD, rows] = jnp.zeros((HEAD_PAD - QK_DIM, qt.shape[1]), BF16)

    def keys():
        kp = v["kp"]
        krot = kp * kcos_ref[rows, :] + pltpu.roll(kp, HEAD_PAD - ROPE, axis=1) * ksin_ref[rows, :]
        k_ref[rows, :] = _dot(jnp.concatenate([v["nkv"], krot.astype(BF16)], axis=1), wk_ref[...]).astype(BF16)

    def values():
        vt = _dot_nt(wvt_ref[...], v["nkv"])
        row = lax.broadcasted_iota(jnp.int32, vt.shape, 0)
        vt_ref[:, rows] = jnp.where(row % V_ROWS == V_DIM, 1.0, vt).astype(BF16)

    return (project, low_rank_norms, queries, keys, values)


def _pre_kernel(*refs, n_groups, split_input):
    t = TOKEN_BLOCK
    if split_input:
        *x_refs, xc_ref = refs[:GROUP + 1]
        refs = refs[GROUP + 1:]
        xs = [r[...] for r in x_refs]
        xs[-1] = jnp.where(pl.program_id(1) == n_groups - 1, xc_ref[...], xs[-1])
    else:
        x_ref, *refs = refs
        xs = [x_ref[j * t:(j + 1) * t, :] for j in range(GROUP)]
    mod_refs, refs = refs[:GROUP], refs[GROUP:]
    nw_ref, wi_ref, wo_ref, *side_refs, xo_ref, qt_ref, k_ref, vt_ref = refs
    side_refs = (*side_refs, qt_ref, k_ref, vt_ref)
    nw = nw_ref[...]

    sides = ()
    for j in range(GROUP):
        rows = slice(j * t, (j + 1) * t)
        mod = mod_refs[j][...]
        x1 = _ffn_half_step(xs[j], _ffn_input(xs[j], nw[0:1], mod[0:1], mod[1:2]), mod[2:3], wi_ref, wo_ref,
                            side=sides)
        xo_ref[rows, :] = x1
        sides = _mla_pieces(x1, mod, nw, rows, side_refs)
    for piece in sides:
        piece()


def _pre(x, x_ctx, mods, norm_w, w, tabs, layer, n_lat_blocks):
    batch = x.shape[0]
    t = TOKEN_BLOCK
    tg = t * GROUP
    split_input = x_ctx is not None
    tokens = x.shape[1] + (x_ctx.shape[1] if split_input else 0)
    assert tokens % tg == 0
    ng = tokens // tg

    if split_input:
        assert x_ctx.shape[1] == t
        x_args = [x] * GROUP + [x_ctx]
        x_specs = [pl.BlockSpec((None, t, D_MODEL),
                                lambda b, i, j=j: (b, jnp.minimum(i * GROUP + j, n_lat_blocks - 1), 0))
                   for j in range(GROUP)]
        x_specs.append(pl.BlockSpec((None, t, D_MODEL), lambda b, i: (b, 0, 0)))
    else:
        x_args = [x]
        x_specs = [pl.BlockSpec((None, tg, D_MODEL), lambda b, i: (b, i, 0))]
    mod_specs = [pl.BlockSpec((None, N_MOD, D_MODEL),
                              lambda b, i, j=j: (jnp.where(i * GROUP + j >= n_lat_blocks, batch, b), 0, 0))
                 for j in range(GROUP)]

    return pl.pallas_call(
        functools.partial(_pre_kernel, n_groups=ng, split_input=split_input),
        grid=(batch, ng),
        in_specs=x_specs + mod_specs + [
            _const_spec((3, D_MODEL)),
            _layer_spec((D_MODEL, 2 * D_FF), layer),
            _layer_spec((D_FF, D_MODEL), layer),
            _layer_spec((D_MODEL, 512), layer),
            _layer_spec((1, Q_RANK), layer),
            _layer_spec((1, KV_RANK), layer),
            _layer_spec((HEADS * HEAD_PAD, Q_RANK), layer),
            _layer_spec((2 * KV_RANK, HEADS * HEAD_PAD), layer),
            _layer_spec((HEADS * V_ROWS, KV_RANK), layer),
            pl.BlockSpec((ROPE, tg), lambda b, i: (0, i)),
            pl.BlockSpec((ROPE, tg), lambda b, i: (0, i)),
            pl.BlockSpec((tg, HEAD_PAD), lambda b, i: (i, 0)),
            pl.BlockSpec((tg, HEAD_PAD), lambda b, i: (i, 0)),
        ],
        out_specs=[
            pl.BlockSpec((None, tg, D_MODEL), lambda b, i: (b, i, 0)),
            pl.BlockSpec((None, HEADS * HEAD_PAD, tg), lambda b, i: (b, 0, i)),
            pl.BlockSpec((None, tg, HEADS * HEAD_PAD), lambda b, i: (b, i, 0)),
            pl.BlockSpec((None, HEADS * V_ROWS, tg), lambda b, i: (b, 0, i)),
        ],
        out_shape=[
            jax.ShapeDtypeStruct((batch, tokens, D_MODEL), F32),
            jax.ShapeDtypeStruct((batch, HEADS * HEAD_PAD, tokens), BF16),
            jax.ShapeDtypeStruct((batch, tokens, HEADS * HEAD_PAD), BF16),
            jax.ShapeDtypeStruct((batch, HEADS * V_ROWS, tokens), BF16),
        ],
        compiler_params=_params(),
        name="pre",
    )(*x_args, *[mods] * GROUP, norm_w, w["wi1"], w["wo1"], w["wqkv"], w["qnw"], w["kvnw"], w["wuqt"], w["wk"],
      w["wvt"], tabs["qcos"], tabs["qsin"], tabs["kcos"], tabs["ksin"])


def _column_max(s, mx):
    parts = [s[r:r + MAX_ROWS] for r in range(0, s.shape[0], MAX_ROWS)]
    if mx is not None:
        parts.append(mx)
    while len(parts) > 1:
        nxt = [jnp.maximum(parts[j], parts[j + 1]) for j in range(0, len(parts) - 1, 2)]
        if len(parts) % 2:
            nxt.append(parts[-1])
        parts = nxt
    return parts[0]


def _scores(k_ref, qt, s_ref, m_ref, n_chunks, key_chunk):
    mx = None
    for c in range(n_chunks):
        rows = slice(c * key_chunk, (c + 1) * key_chunk)
        s = _dot(k_ref[rows, :], qt)
        s_ref[rows, :] = s
        mx = _column_max(s, mx)
    m_ref[...] = jnp.max(mx, axis=0, keepdims=True)


def _attn_kernel(qt0_ref, k0_ref, qtn_ref, kn_ref, vt_ref, o_ref, s_ref, m_ref, *, n_keys, key_chunk):
    n_chunks = n_keys // key_chunk
    tq = qtn_ref.shape[1]
    first = jnp.logical_and(jnp.logical_and(pl.program_id(0) == 0, pl.program_id(1) == 0),
                            pl.program_id(2) == 0)

    @pl.when(first)
    def _():
        _scores(k0_ref, qt0_ref[...], s_ref, m_ref, n_chunks, key_chunk)

    m = m_ref[...]
    qtn = qtn_ref[...]
    acc = jnp.zeros((V_ROWS, tq), F32)
    mx = None
    for c in range(n_chunks):
        rows = slice(c * key_chunk, (c + 1) * key_chunk)
        p = jnp.exp2(s_ref[rows, :] - m).astype(BF16)
        s = _dot(kn_ref[rows, :], qtn)
        acc = acc + _dot(vt_ref[:, rows], p)
        s_ref[rows, :] = s
        mx = _column_max(s, mx)
    m_ref[...] = jnp.max(mx, axis=0, keepdims=True)
    o_ref[...] = (acc[:V_DIM] * (1.0 / acc[V_DIM:V_DIM + 1])).astype(BF16)


def _attn(qt, k, vt, *, n_q_blocks):
    batch, _, tokens = qt.shape
    tq = QUERY_BLOCK
    assert tokens % KEY_CHUNK == 0

    def nxt(b, h, i):
        wrap_i = i == n_q_blocks - 1
        wrap_h = jnp.logical_and(wrap_i, h == HEADS - 1)
        last = jnp.logical_and(wrap_h, b == batch - 1)
        i2 = jnp.where(wrap_i, 0, i + 1)
        h2 = jnp.where(wrap_i, jnp.where(h == HEADS - 1, 0, h + 1), h)
        b2 = jnp.where(wrap_h, b + 1, b)
        return jnp.where(last, b, b2), jnp.where(last, h, h2), jnp.where(last, i, i2)

    def qt_next(b, h, i):
        b2, h2, i2 = nxt(b, h, i)
        return (b2, h2, i2)

    def k_next(b, h, i):
        b2, h2, _ = nxt(b, h, i)
        return (b2, 0, h2)

    kernel = functools.partial(_attn_kernel, n_keys=tokens, key_chunk=KEY_CHUNK)
    return pl.pallas_call(
        kernel,
        grid=(batch, HEADS, n_q_blocks),
        in_specs=[
            pl.BlockSpec((None, HEAD_PAD, tq), lambda b, h, i: (0, 0, 0), pipeline_mode=pl.Buffered(1)),
            pl.BlockSpec((None, tokens, HEAD_PAD), lambda b, h, i: (0, 0, 0), pipeline_mode=pl.Buffered(1)),
            pl.BlockSpec((None, HEAD_PAD, tq), qt_next),
            pl.BlockSpec((None, tokens, HEAD_PAD), k_next),
            pl.BlockSpec((None, V_ROWS, tokens), lambda b, h, i: (b, h, 0)),
        ],
        out_specs=pl.BlockSpec((None, V_DIM, tq), lambda b, h, i: (b, h, i)),
        out_shape=jax.ShapeDtypeStruct((batch, HEADS * V_DIM, n_q_blocks * tq), BF16),
        scratch_shapes=[pltpu.VMEM((tokens, tq), F32), pltpu.VMEM((1, tq), F32)],
        compiler_params=pltpu.CompilerParams(
            dimension_semantics=("arbitrary", "arbitrary", "arbitrary"), vmem_limit_bytes=VMEM_LIMIT_BYTES),
        name="attn",
    )(qt, k, qt, k, vt)


def _attn_ctx_kernel(qt_ref, k_ref, vt_ref, o_ref):
    for hd in range(ATTN_HEADS_PER_STEP):
        s = _dot(k_ref[:, hd * HEAD_PAD:(hd + 1) * HEAD_PAD], qt_ref[hd * HEAD_PAD:(hd + 1) * HEAD_PAD, :])
        p = jnp.exp2(s - jnp.max(s, axis=0, keepdims=True)).astype(BF16)
        acc = _dot(vt_ref[hd * V_ROWS:(hd + 1) * V_ROWS, :], p)
        o_ref[hd * V_DIM:(hd + 1) * V_DIM, :] = (acc[:V_DIM] * (1.0 / acc[V_DIM:V_DIM + 1])).astype(BF16)


def _attn_ctx(qt, k, vt, *, n_ctx):
    batch, _, tokens = qt.shape
    hb = ATTN_HEADS_PER_STEP
    assert tokens % n_ctx == 0
    c = tokens // n_ctx - 1
    return pl.pallas_call(
        _attn_ctx_kernel,
        grid=(batch, HEADS // hb),
        in_specs=[
            pl.BlockSpec((None, hb * HEAD_PAD, n_ctx), lambda b, g: (b, g, c)),
            pl.BlockSpec((None, n_ctx, hb * HEAD_PAD), lambda b, g: (b, c, g)),
            pl.BlockSpec((None, hb * V_ROWS, n_ctx), lambda b, g: (b, g, c)),
        ],
        out_specs=pl.BlockSpec((None, hb * V_DIM, n_ctx), lambda b, g: (b, g, 0)),
        out_shape=jax.ShapeDtypeStruct((batch, HEADS * V_DIM, n_ctx), BF16),
        compiler_params=_params(),
        name="attn_ctx",
    )(qt, k, vt)


def _post_kernel(*refs, group, n_lat_blocks, n_blocks, has_ctx, final_norm):
    t = TOKEN_BLOCK
    i = pl.program_id(1)
    x_ref, xp_ref, xn_ref, *refs = refs
    att_refs, refs = refs[:group], refs[group:]
    if has_ctx:
        attc_ref, *refs = refs
    mod_refs, refs = refs[:group], refs[group:]
    nw_ref, wcv_ref, cw_ref, wout_ref, wi_ref, wo_ref, fnw_ref, o_ref, xe_ref, u_ref = refs
    nw = nw_ref[...]
    cw = cw_ref[...]

    def mixer_pieces(j):
        blk = i * group + j
        lo = j * t
        x = x_ref[lo:lo + t, :]
        mod = mod_refs[j][...]
        v = {}

        def conv_project():
            xe_ref[0:HALO, :] = xp_ref[...] if j == 0 else x_ref[lo - HALO:lo, :]
            xe_ref[HALO:HALO + t, :] = x
            xe_ref[HALO + t:, :] = xn_ref[...] if j == group - 1 else x_ref[lo + t:lo + t + HALO, :]
            h = _rmsnorm(xe_ref[...], nw[1:2]) * (1.0 + mod[4:5]) + mod[3:4]
            v["pj"] = _dot(h.astype(BF16), wcv_ref[...])

        def gated_conv():
            pj = v["pj"]
            u_ref[...] = pj[:, CONV_DIM:2 * CONV_DIM] * pj[:, 2 * CONV_DIM:]
            left_ok = jnp.logical_and(blk != 0, blk != n_lat_blocks)
            right_ok = jnp.logical_and(blk != n_lat_blocks - 1, blk != n_blocks - 1)
            u_ref[0:HALO, :] = jnp.where(left_ok, u_ref[0:HALO, :], 0.0)
            u_ref[HALO + t:, :] = jnp.where(right_ok, u_ref[HALO + t:, :], 0.0)
            conv = (u_ref[HALO - 1:HALO - 1 + t, :] * cw[0:1] + u_ref[HALO:HALO + t, :] * cw[1:2]
                    + u_ref[HALO + 1:HALO + 1 + t, :] * cw[2:3])
            v["conv"] = (pj[HALO:HALO + t, :CONV_DIM] * conv).astype(BF16)

        def mix_out():
            att = att_refs[j][...]
            if has_ctx and j == group - 1:
                att = jnp.where(blk >= n_lat_blocks, attc_ref[...], att)
            mix = _dot(jnp.concatenate([att.T, v["conv"]], axis=1), wout_ref[...])
            x1 = x + mod[5:6] * mix
            v["x1"] = x1
            v["h"] = _ffn_input(x1, nw[2:3], mod[6:7], mod[7:8])

        return v, (conv_project, gated_conv, mix_out)

    v, pieces = mixer_pieces(0)
    for piece in pieces:
        piece()
    for j in range(group):
        nxt_v, sides = mixer_pieces(j + 1) if j + 1 < group else (None, ())
        x2 = _ffn_half_step(v["x1"], v["h"], mod_refs[j][...][8:9], wi_ref, wo_ref, side=sides)
        if final_norm:
            x2 = _rmsnorm(x2, fnw_ref[...])
        o_ref[j * t:(j + 1) * t, :] = x2
        v = nxt_v


def _post(x, att, att_ctx, mods, norm_w, w, final_norm_w, *, layer, n_lat_blocks, n_out_blocks, final_norm):
    batch, tokens, _ = x.shape
    t = TOKEN_BLOCK
    group = GROUP if n_out_blocks % GROUP == 0 else LATENT_GROUP
    assert n_out_blocks % group == 0
    tg = t * group
    ng = n_out_blocks // group
    halo_per_step = tg // HALO
    n_halo = tokens // HALO
    has_ctx = att_ctx is not None

    att_args = [att] * group
    att_specs = [pl.BlockSpec((None, HEADS * V_DIM, t),
                              lambda b, i, j=j: (b, 0, jnp.minimum(i * group + j, n_lat_blocks - 1)))
                 for j in range(group)]
    if has_ctx:
        assert att_ctx.shape[2] == t
        att_args.append(att_ctx)
        att_specs.append(pl.BlockSpec((None, HEADS * V_DIM, t), lambda b, i: (b, 0, 0)))
    mod_specs = [pl.BlockSpec((None, N_MOD, D_MODEL),
                              lambda b, i, j=j: (jnp.where(i * group + j >= n_lat_blocks, batch, b), 0, 0))
                 for j in range(group)]
    kernel = functools.partial(_post_kernel, group=group, n_lat_blocks=n_lat_blocks, n_blocks=tokens // t,
                               has_ctx=has_ctx, final_norm=final_norm)
    return pl.pallas_call(
        kernel,
        grid=(batch, ng),
        in_specs=[
            pl.BlockSpec((None, tg, D_MODEL), lambda b, i: (b, i, 0)),
            pl.BlockSpec((None, HALO, D_MODEL), lambda b, i: (b, jnp.maximum(i * halo_per_step - 1, 0), 0)),
            pl.BlockSpec((None, HALO, D_MODEL),
                         lambda b, i: (b, jnp.minimum((i + 1) * halo_per_step, n_halo - 1), 0)),
        ] + att_specs + mod_specs + [
            _const_spec((3, D_MODEL)),
            _layer_spec((D_MODEL, 3 * CONV_DIM), layer),
            _layer_spec((3, CONV_DIM), layer),
            _layer_spec((HEADS * V_DIM + CONV_DIM, D_MODEL), layer),
            _layer_spec((D_MODEL, 2 * D_FF), layer),
            _layer_spec((D_FF, D_MODEL), layer),
            _const_spec((1, D_MODEL)),
        ],
        out_specs=pl.BlockSpec((None, tg, D_MODEL), lambda b, i: (b, i, 0)),
        out_shape=jax.ShapeDtypeStruct((batch, n_out_blocks * t, D_MODEL), F32),
        scratch_shapes=[pltpu.VMEM((t + 2 * HALO, D_MODEL), F32), pltpu.VMEM((t + 2 * HALO, CONV_DIM), F32)],
        compiler_params=_params(),
        name="post",
    )(x, x, x, *att_args, *[mods] * group, norm_w, w["wcv"], w["cw"], w["wout"], w["wi2"], w["wo2"],
      final_norm_w)


def _mixer_weights(w_in, q_norm_w, kv_norm_w, w_uq, w_ukv, conv_w):
    depth = w_in.shape[0]
    partner = jnp.array(_ROPE_PARTNER)
    kv0 = Q_RANK + KV_RANK
    kpe = w_in[:, :, kv0:kv0 + ROPE]
    wqkv = jnp.concatenate(
        [w_in[:, :, :kv0], kpe, kpe[:, :, partner], jnp.zeros((depth, D_MODEL, HEAD_PAD - 2 * ROPE), F32)], axis=2)

    uq = w_uq.reshape(depth, Q_RANK, HEADS, QK_DIM)
    uq = jnp.concatenate([uq, uq[..., NOPE:][..., partner]], axis=3)
    wuqt = jnp.swapaxes(uq.reshape(depth, Q_RANK, HEADS * HEAD_PAD), 1, 2)

    ukv = w_ukv.reshape(depth, KV_RANK, HEADS, NOPE + V_DIM)
    wk_nope = jnp.concatenate(
        [ukv[..., :NOPE], jnp.zeros((depth, KV_RANK, HEADS, HEAD_PAD - NOPE), F32)], axis=3)
    place = np.zeros((depth, KV_RANK, HEADS, HEAD_PAD), np.float32)
    place[:, np.arange(ROPE), :, NOPE + np.arange(ROPE)] = 1.0
    wk = jnp.concatenate([wk_nope, jnp.asarray(place)], axis=1).reshape(depth, 2 * KV_RANK, HEADS * HEAD_PAD)

    wv = jnp.concatenate([ukv[..., NOPE:], jnp.zeros((depth, KV_RANK, HEADS, V_ROWS - V_DIM), F32)], axis=3)
    wvt = jnp.swapaxes(wv.reshape(depth, KV_RANK, HEADS * V_ROWS), 1, 2)

    return {
        "wqkv": wqkv.astype(BF16), "wcv": w_in[:, :, kv0 + ROPE:].astype(BF16),
        "qnw": q_norm_w.reshape(depth, 1, Q_RANK), "kvnw": kv_norm_w.reshape(depth, 1, KV_RANK),
        "wuqt": wuqt.astype(BF16), "wk": wk.astype(BF16), "wvt": wvt.astype(BF16), "cw": conv_w,
    }


def _rope_tables(n_lat, n_ctx):
    t = np.arange(n_lat)
    row = (t // GRID_W).astype(np.float32)
    col = (t % GRID_W).astype(np.float32)
    d_axis = ROPE // 2
    inv = (ROPE_BASE ** (-np.arange(0, d_axis, 2, dtype=np.float32) / d_axis)).astype(np.float32)
    ar = row[:, None] * inv
    ac = col[:, None] * inv
    cr, sr, cc, sc = np.cos(ar), np.sin(ar), np.cos(ac), np.sin(ac)
    cos = np.concatenate([cr, cr, cc, cc], axis=1)
    sin = np.concatenate([-sr, sr, -sc, sc], axis=1)
    cos = np.concatenate([cos, np.ones((n_ctx, ROPE), np.float32)], axis=0)
    sin = np.concatenate([sin, np.zeros((n_ctx, ROPE), np.float32)], axis=0)
    pad = np.zeros((n_lat + n_ctx, HEAD_PAD - ROPE), np.float32)
    tabs = {
        "qcos": (cos * Q_SCALE).T, "qsin": (sin * Q_SCALE).T,
        "kcos": np.concatenate([cos, pad], axis=1), "ksin": np.concatenate([sin, pad], axis=1),
    }
    return {name: jnp.asarray(np.ascontiguousarray(v, dtype=np.float32)) for name, v in tabs.items()}


def kernel(x, c, ctx, c_ctx, w_ada, b_ada, norm_w, w_ffn1_in, w_ffn1_out, w_ffn2_in, w_ffn2_out, w_in, q_norm_w,
           kv_norm_w, w_uq, w_ukv, conv_w, w_out, final_norm_w):
    batch, n_lat, _ = x.shape
    n_ctx = ctx.shape[1]
    depth = w_ada.shape[0]
    assert n_ctx % TOKEN_BLOCK == 0 and n_lat % QUERY_BLOCK == 0 and QUERY_BLOCK % TOKEN_BLOCK == 0
    assert batch + 1 <= 8
    n_lat_blocks = n_lat // TOKEN_BLOCK
    n_blocks = (n_lat + n_ctx) // TOKEN_BLOCK

    cond = jnp.concatenate([c, c_ctx[None, :], jnp.zeros((8 - batch - 1, D_MODEL), F32)], axis=0)
    mods = _ada(cond, w_ada, b_ada).reshape(depth, 8, N_MOD, D_MODEL)
    tabs = _rope_tables(n_lat, n_ctx)
    fnw = final_norm_w.reshape(1, D_MODEL)
    w = {"wi1": w_ffn1_in.astype(BF16), "wo1": w_ffn1_out.astype(BF16), "wi2": w_ffn2_in.astype(BF16),
         "wo2": w_ffn2_out.astype(BF16), "wout": w_out.astype(BF16)}
    w.update(_mixer_weights(w_in, q_norm_w, kv_norm_w, w_uq, w_ukv, conv_w))

    xs, xs_ctx = x, ctx
    for l in range(depth):
        last = l == depth - 1
        xs, qt, k, vt = _pre(xs, xs_ctx, mods[l], norm_w[l], w, tabs, l, n_lat_blocks)
        xs_ctx = None
        att = _attn(qt, k, vt, n_q_blocks=n_lat // QUERY_BLOCK)
        att_ctx = None if last else _attn_ctx(qt, k, vt, n_ctx=n_ctx)
        xs = _post(xs, att, att_ctx, mods[l], norm_w[l], w, fnw, layer=l, n_lat_blocks=n_lat_blocks,
                   n_out_blocks=n_lat_blocks if last else n_blocks, final_norm=last)
    return xs
```
